```python
import math
import jax, jax.numpy as jnp
from jax import lax
import numpy as np

D_MODEL = 2048
BATCH = 8
SEQ = 2048
DEPTH = 1

HEAD_DIM = 128
N_HEADS_SB = 8
N_HEADS_MOBA = 8
W_SB = N_HEADS_SB * HEAD_DIM
W_MOBA = N_HEADS_MOBA * HEAD_DIM
W_MIX = W_SB + W_MOBA
SB_BLOCK = 128
MOBA_BLOCK = 256
MOBA_TOP_K = 3
MOBA_Q_CHUNK = 16
ROPE_DIM = HEAD_DIM // 4
ROPE_THETA = 500000.0
N_MEM = 256
N_HEADS_X = 4
W_X = N_HEADS_X * HEAD_DIM
D_FF = ((8 * D_MODEL + 3 * 256 - 1) // (3 * 256)) * 256
EPS = 1e-6
ATTN_SCALE = HEAD_DIM ** -0.5

kernel_name = "hybrid_stickbreak_moba_memxattn_swiglu"


def rmsnorm(x, g):
    xf = x.astype(jnp.float32)
    y = xf * lax.rsqrt(jnp.mean(xf * xf, axis=-1, keepdims=True) + EPS)
    return (y * g.astype(jnp.float32)).astype(x.dtype)


def partial_rotary(x, positions):
    half = ROPE_DIM // 2
    inv_freq = ROPE_THETA ** (-jnp.arange(0, ROPE_DIM, 2, dtype=jnp.float32) / ROPE_DIM)
    ang = positions.astype(jnp.float32)[:, None, :, None] * inv_freq
    cos, sin = jnp.cos(ang), jnp.sin(ang)
    xf = x.astype(jnp.float32)
    x1, x2, rest = xf[..., :half], xf[..., half:ROPE_DIM], xf[..., ROPE_DIM:]
    out = jnp.concatenate([x1 * cos - x2 * sin, x2 * cos + x1 * sin, rest], axis=-1)
    return out.astype(x.dtype)


def stick_breaking_attention(q, k, v):
    S = q.shape[2]
    outs = []
    for i in range(S // SB_BLOCK):
        t0, t1 = i * SB_BLOCK, (i + 1) * SB_BLOCK
        qb = q[:, :, t0:t1].astype(jnp.float32)
        kb = k[:, :, :t1].astype(jnp.float32)
        vb = v[:, :, :t1]
        z = jnp.einsum('bhtd,bhsd->bhts', qb, kb) * ATTN_SCALE
        t_pos = t0 + jnp.arange(SB_BLOCK)[:, None]
        s_pos = jnp.arange(t1)[None, :]
        past = s_pos < t_pos
        log_1m_beta = jnp.where(past, jax.nn.log_sigmoid(-z), 0.0)
        after = lax.cumsum(log_1m_beta, axis=3, reverse=True) - log_1m_beta
        w = jnp.where(past, jnp.exp(jax.nn.log_sigmoid(z) + after), 0.0)
        outs.append(jnp.einsum('bhts,bhsd->bhtd', w.astype(vb.dtype), vb))
    return jnp.concatenate(outs, axis=2)


def moba_attention(q, k, v):
    B, H, S, d = q.shape
    NB = -(-S // MOBA_BLOCK)
    pad = NB * MOBA_BLOCK - S
    kp = jnp.pad(k, ((0, 0), (0, 0), (0, pad), (0, 0)))
    vp = jnp.pad(v, ((0, 0), (0, 0), (0, pad), (0, 0)))
    kblk = kp.reshape(B, H, NB, MOBA_BLOCK, d)
    vblk = vp.reshape(B, H, NB, MOBA_BLOCK, d)
    kmean = jnp.mean(kblk.astype(jnp.float32), axis=3)
    n_sel = min(MOBA_TOP_K, NB)
    bi = jnp.arange(B)[:, None, None, None]
    hi = jnp.arange(H)[None, :, None, None]

    def chunk(ci):
        t0 = ci * MOBA_Q_CHUNK
        blk = t0 // MOBA_BLOCK
        qc = lax.dynamic_slice_in_dim(q, t0, MOBA_Q_CHUNK, axis=2).astype(jnp.float32)
        gate = jnp.einsum('bhqd,bhnd->bhqn', qc, kmean)
        fully_past = jnp.arange(NB) < blk
        gate = jnp.where(fully_past, gate, -jnp.inf)
        top_val, top_idx = lax.top_k(gate, n_sel)
        valid = jnp.isfinite(top_val)
        kg = kblk[bi, hi, top_idx]
        vg = vblk[bi, hi, top_idx]
        s_sel = jnp.einsum('bhqd,bhqncd->bhqnc', qc, kg.astype(jnp.float32)) * ATTN_SCALE
        s_sel = jnp.where(valid[..., None], s_sel, -jnp.inf)
        s_sel = s_sel.reshape(B, H, MOBA_Q_CHUNK, n_sel * MOBA_BLOCK)
        k_own = lax.dynamic_slice_in_dim(kp, blk * MOBA_BLOCK, MOBA_BLOCK, axis=2)
        v_own = lax.dynamic_slice_in_dim(vp, blk * MOBA_BLOCK, MOBA_BLOCK, axis=2)
        s_own = jnp.einsum('bhqd,bhcd->bhqc', qc, k_own.astype(jnp.float32)) * ATTN_SCALE
        key_pos = blk * MOBA_BLOCK + jnp.arange(MOBA_BLOCK)
        q_pos = t0 + jnp.arange(MOBA_Q_CHUNK)
        s_own = jnp.where(key_pos[None, :] <= q_pos[:, None], s_own, -jnp.inf)
        p = jax.nn.softmax(jnp.concatenate([s_sel, s_own], axis=-1), axis=-1)
        p_sel = p[..., :n_sel * MOBA_BLOCK].reshape(B, H, MOBA_Q_CHUNK, n_sel, MOBA_BLOCK)
        p_own = p[..., n_sel * MOBA_BLOCK:]
        return (jnp.einsum('bhqnc,bhqncd->bhqd', p_sel.astype(v.dtype), vg)
                + jnp.einsum('bhqc,bhcd->bhqd', p_own.astype(v.dtype), v_own))

    outs = lax.map(chunk, jnp.arange(S // MOBA_Q_CHUNK))
    return outs.transpose(1, 2, 0, 3, 4).reshape(B, H, S, d)


def split_heads(t, n_heads):
    B, S, _ = t.shape
    return t.reshape(B, S, n_heads, HEAD_DIM).transpose(0, 2, 1, 3)


def parallel_mixer(h, positions, w_in, g_out_sb, g_out_moba, w_out):
    B, S, _ = h.shape
    proj = h @ w_in
    q_sb, k_sb, v_sb, q_mb, k_mb, v_mb = jnp.split(
        proj, [W_SB, 2 * W_SB, 3 * W_SB, 3 * W_SB + W_MOBA, 3 * W_SB + 2 * W_MOBA], axis=-1)
    o_sb = stick_breaking_attention(split_heads(q_sb, N_HEADS_SB),
                                    split_heads(k_sb, N_HEADS_SB),
                                    split_heads(v_sb, N_HEADS_SB))
    q_mb = partial_rotary(split_heads(q_mb, N_HEADS_MOBA), positions)
    k_mb = partial_rotary(split_heads(k_mb, N_HEADS_MOBA), positions)
    o_mb = moba_attention(q_mb, k_mb, split_heads(v_mb, N_HEADS_MOBA))
    o_sb = rmsnorm(o_sb.transpose(0, 2, 1, 3), g_out_sb.reshape(N_HEADS_SB, HEAD_DIM))
    o_mb = rmsnorm(o_mb.transpose(0, 2, 1, 3), g_out_moba.reshape(N_HEADS_MOBA, HEAD_DIM))
    o = jnp.concatenate([o_sb.reshape(B, S, W_SB), o_mb.reshape(B, S, W_MOBA)], axis=-1)
    return o @ w_out


def memory_cross_attention(h, m, w_xq, w_xkv, w_xo):
    B, S, _ = h.shape
    q = split_heads(h @ w_xq, N_HEADS_X).astype(jnp.float32)
    kv = (m @ w_xkv).reshape(B, N_MEM, 2, N_HEADS_X, HEAD_DIM)
    k = kv[:, :, 0].transpose(0, 2, 1, 3)
    v = kv[:, :, 1].transpose(0, 2, 1, 3)
    s = jnp.einsum('bhsd,bhmd->bhsm', q, k.astype(jnp.float32)) * ATTN_SCALE
    p = jax.nn.softmax(s, axis=-1)
    o = jnp.einsum('bhsm,bhmd->bhsd', p.astype(v.dtype), v)
    return o.transpose(0, 2, 1, 3).reshape(B, S, W_X) @ w_xo


def swiglu(h, w_gate_up, w_down):
    gu = h @ w_gate_up
    g, u = jnp.split(gu, 2, axis=-1)
    return (jax.nn.silu(g) * u) @ w_down


def setup_inputs(seed: int = 0) -> dict:
    key = jax.random.key(seed)
    ks = jax.random.split(key, 16)

    def w(k, fan_in, fan_out):
        return jax.random.normal(k, (fan_in, fan_out), jnp.float32) * fan_in ** -0.5

    def gain(k, n):
        return 1.0 + 0.01 * jax.random.normal(k, (n,), jnp.float32)

    x = jax.random.normal(ks[0], (BATCH, SEQ, D_MODEL), jnp.float32)
    mem = jax.random.normal(ks[1], (BATCH, N_MEM, D_MODEL), jnp.float32)
    offset = jax.random.randint(ks[2], (BATCH, 1), 0, 4096, dtype=jnp.int32)
    positions = offset + jnp.arange(SEQ, dtype=jnp.int32)[None, :]
    return {
        "x": x,
        "mem": mem,
        "positions": positions,
        "g_mix": gain(ks[3], D_MODEL),
        "w_in": w(ks[4], D_MODEL, 3 * W_SB + 3 * W_MOBA),
        "g_out_sb": gain(ks[5], W_SB),
        "g_out_moba": gain(ks[6], W_MOBA),
        "w_out": w(ks[7], W_MIX, D_MODEL),
        "g_xattn": gain(ks[8], D_MODEL),
        "g_mem": gain(ks[9], D_MODEL),
        "w_xq": w(ks[10], D_MODEL, W_X),
        "w_xkv": w(ks[11], D_MODEL, 2 * W_X),
        "w_xo": w(ks[12], W_X, D_MODEL),
        "g_ffn": gain(ks[13], D_MODEL),
        "w_gate_up": w(ks[14], D_MODEL, 2 * D_FF),
        "w_down": w(ks[15], D_FF, D_MODEL),
        "g_final": gain(jax.random.fold_in(key, 99), D_MODEL),
    }


def reference(x, mem, positions, g_mix, w_in, g_out_sb, g_out_moba, w_out,
              g_xattn, g_mem, w_xq, w_xkv, w_xo, g_ffn, w_gate_up, w_down, g_final):
    m = rmsnorm(mem, g_mem)
    for _ in range(DEPTH):
        x = x + parallel_mixer(rmsnorm(x, g_mix), positions, w_in, g_out_sb, g_out_moba, w_out)
        x = x + memory_cross_attention(rmsnorm(x, g_xattn), m, w_xq, w_xkv, w_xo)
        x = x + swiglu(rmsnorm(x, g_ffn), w_gate_up, w_down)
    return rmsnorm(x, g_final)
```

```python
import functools

import jax
import jax.numpy as jnp
from jax import lax
from jax.experimental import pallas as pl
from jax.experimental.pallas import tpu as pltpu

F32 = jnp.float32
BF16 = jnp.bfloat16

HEAD_DIM = 128
N_HEADS_SB = 8
N_HEADS_MOBA = 8
N_HEADS_X = 4
MOBA_BLOCK = 256
MOBA_TOP_K = 3
ROPE_DIM = HEAD_DIM // 4
ROPE_THETA = 500000.0
EPS = 1e-6
ATTN_SCALE = HEAD_DIM ** -0.5

V7X_VMEM_BYTES = 64 * 1024 * 1024
VMEM_LIMIT_BYTES = V7X_VMEM_BYTES * 7 // 8

NORM_ROWS = 512
MM_TM = 1024
MM_TN = 1024
FFN_TN = 512
ATT_TQ = 256
ATT_TK = 256
XATT_TQ = 512
NEG_BIG = -1e30

_NT = (((1,), (1,)), ((), ()))


def _params(n_axes):
    return pltpu.CompilerParams(
        dimension_semantics=("arbitrary",) * n_axes,
        vmem_limit_bytes=VMEM_LIMIT_BYTES)


def _dot(a, b):
    return jnp.dot(a, b, preferred_element_type=F32)


def _dot_nt(a, b):
    return lax.dot_general(a, b, _NT, preferred_element_type=F32)


def _rmsnorm_rows(x, g):
    ms = jnp.mean(x * x, axis=-1, keepdims=True)
    return x * lax.rsqrt(ms + EPS) * g


def _rmsnorm_kernel(x_ref, g_ref, o_ref):
    o_ref[...] = _rmsnorm_rows(x_ref[...].astype(F32), g_ref[...]).astype(o_ref.dtype)


def _rmsnorm(x, g, out_dtype):
    n, d = x.shape
    return pl.pallas_call(
        _rmsnorm_kernel,
        out_shape=jax.ShapeDtypeStruct((n, d), out_dtype),
        grid=(n // NORM_ROWS,),
        in_specs=[pl.BlockSpec((NORM_ROWS, d), lambda i: (i, 0)),
                  pl.BlockSpec((1, d), lambda i: (0, 0))],
        out_specs=pl.BlockSpec((NORM_ROWS, d), lambda i: (i, 0)),
        compiler_params=_params(1),
        name="rmsnorm",
    )(x, g.reshape(1, d))


def _matmul_kernel(a_ref, w_ref, o_ref):
    o_ref[...] = _dot(a_ref[...], w_ref[...]).astype(o_ref.dtype)


def _matmul_res_kernel(a_ref, w_ref, r_ref, o_ref):
    o_ref[...] = (r_ref[...] + _dot(a_ref[...], w_ref[...])).astype(o_ref.dtype)


def _matmul(a, w, out_dtype, residual=None, tm=MM_TM, tn=MM_TN, name="matmul"):
    m, k = a.shape
    n = w.shape[1]
    tm, tn = min(tm, m), min(tn, n)
    in_specs = [pl.BlockSpec((tm, k), lambda i, j: (i, 0)),
                pl.BlockSpec((k, tn), lambda i, j: (0, j))]
    args = [a, w]
    body = _matmul_kernel
    if residual is not None:
        in_specs.append(pl.BlockSpec((tm, tn), lambda i, j: (i, j)))
        args.append(residual)
        body = _matmul_res_kernel
    return pl.pallas_call(
        body,
        out_shape=jax.ShapeDtypeStruct((m, n), out_dtype),
        grid=(m // tm, n // tn),
        in_specs=in_specs,
        out_specs=pl.BlockSpec((tm, tn), lambda i, j: (i, j)),
        compiler_params=_params(2),
        name=name,
    )(*args)


def _mix_out_kernel(a1_ref, a2_ref, w1_ref, w2_ref, r_ref, o_ref):
    acc = _dot(a1_ref[...], w1_ref[...]) + _dot(a2_ref[...], w2_ref[...])
    o_ref[...] = r_ref[...] + acc


def _mix_out(o_sb, o_mb, w_out, residual):
    m, k1 = o_sb.shape
    k2 = o_mb.shape[1]
    n = w_out.shape[1]
    assert k1 == k2
    tm, tn = MM_TM, MM_TN
    return pl.pallas_call(
        _mix_out_kernel,
        out_shape=jax.ShapeDtypeStruct((m, n), F32),
        grid=(m // tm, n // tn),
        in_specs=[pl.BlockSpec((tm, k1), lambda i, j: (i, 0)),
                  pl.BlockSpec((tm, k2), lambda i, j: (i, 0)),
                  pl.BlockSpec((k1, tn), lambda i, j: (0, j)),
                  pl.BlockSpec((k2, tn), lambda i, j: (1, j)),
                  pl.BlockSpec((tm, tn), lambda i, j: (i, j))],
        out_specs=pl.BlockSpec((tm, tn), lambda i, j: (i, j)),
        compiler_params=_params(2),
        name="mix_out",
    )(o_sb, o_mb, w_out, w_out, residual)


def _swiglu_kernel(a_ref, wg_ref, wu_ref, o_ref):
    a = a_ref[...]
    g = _dot(a, wg_ref[...])
    u = _dot(a, wu_ref[...])
    o_ref[...] = (g / (1.0 + jnp.exp(-g)) * u).astype(o_ref.dtype)


def _swiglu_up(h, w_gate_up):
    m, k = h.shape
    d_ff = w_gate_up.shape[1] // 2
    tm, tn = MM_TM, FFN_TN
    nj = d_ff // tn
    return pl.pallas_call(
        _swiglu_kernel,
        out_shape=jax.ShapeDtypeStruct((m, d_ff), BF16),
        grid=(m // tm, nj),
        in_specs=[pl.BlockSpec((tm, k), lambda i, j: (i, 0)),
                  pl.BlockSpec((k, tn), lambda i, j: (0, j)),
                  pl.BlockSpec((k, tn), lambda i, j: (0, j + nj))],
        out_specs=pl.BlockSpec((tm, tn), lambda i, j: (i, j)),
        compiler_params=_params(2),
        name="swiglu_up",
    )(h, w_gate_up, w_gate_up)


def _rope_table_kernel(pos_ref, invf_ref, sign_ref, cos_ref, sin_ref):
    ang = pos_ref[0].astype(F32) * invf_ref[...]
    cos_ref[0] = jnp.cos(ang)
    sin_ref[0] = jnp.sin(ang) * sign_ref[...]


def _rope_tables(positions):
    b, s = positions.shape
    half = ROPE_DIM // 2
    inv_freq = ROPE_THETA ** (-jnp.arange(0, ROPE_DIM, 2, dtype=F32) / ROPE_DIM)
    pad = jnp.zeros((HEAD_DIM - ROPE_DIM,), F32)
    invf = jnp.concatenate([inv_freq, inv_freq, pad]).reshape(1, HEAD_DIM)
    sign = jnp.concatenate([-jnp.ones((half,), F32), jnp.ones((half,), F32), pad]).reshape(1, HEAD_DIM)
    tab = jax.ShapeDtypeStruct((b, s, HEAD_DIM), F32)
    return pl.pallas_call(
        _rope_table_kernel,
        out_shape=(tab, tab),
        grid=(b,),
        in_specs=[pl.BlockSpec((1, s, 1), lambda i: (i, 0, 0)),
                  pl.BlockSpec((1, HEAD_DIM), lambda i: (0, 0)),
                  pl.BlockSpec((1, HEAD_DIM), lambda i: (0, 0))],
        out_specs=(pl.BlockSpec((1, s, HEAD_DIM), lambda i: (i, 0, 0)),
                   pl.BlockSpec((1, s, HEAD_DIM), lambda i: (i, 0, 0))),
        compiler_params=_params(1),
        name="rope_tables",
    )(positions.reshape(b, s, 1), invf, sign)


def _rotate(x, cos, sin_signed):
    half = ROPE_DIM // 2
    lane = lax.broadcasted_iota(jnp.int32, x.shape, 1)
    partner = jnp.where(lane < half,
                        pltpu.roll(x, HEAD_DIM - half, axis=1),
                        pltpu.roll(x, half, axis=1))
    return x * cos + partner * sin_signed


def _sb_kernel(q_ref, k_ref, v_ref, g_ref, o_ref):
    i = pl.program_id(2)
    tq, tk = ATT_TQ, ATT_TK
    q = q_ref[0]
    row = lax.broadcasted_iota(jnp.int32, (tq, tk), 0)
    col = lax.broadcasted_iota(jnp.int32, (tq, tk), 1)
    past = col < row
    later = (row > col).astype(BF16)

    def tile(j, carry, acc, diagonal):
        start = pl.multiple_of(j * tk, tk)
        kj = k_ref[0, pl.ds(start, tk), :]
        vj = v_ref[0, pl.ds(start, tk), :]
        z = _dot_nt(q, kj) * ATTN_SCALE
        lmb = jnp.minimum(-z, 0.0) - jnp.log(1.0 + jnp.exp(-jnp.abs(z)))
        if diagonal:
            lmb = jnp.where(past, lmb, 0.0)
        after = _dot(lmb.astype(BF16), later) + carry
        w = jnp.exp(z + lmb + after)
        if diagonal:
            w = jnp.where(past, w, 0.0)
        acc = acc + _dot(w.astype(BF16), vj)
        carry = carry + jnp.sum(lmb, axis=-1, keepdims=True)
        return carry, acc

    carry0 = jnp.zeros((tq, 1), F32)
    acc0 = jnp.zeros((tq, HEAD_DIM), F32)
    carry, acc = tile(i, carry0, acc0, True)

    def body(t, state):
        return tile(i - 1 - t, state[0], state[1], False)

    carry, acc = lax.fori_loop(0, i, body, (carry, acc))
    o_ref[0] = _rmsnorm_rows(acc, g_ref[...]).astype(o_ref.dtype)


def _sb_attention(qkv, g_out):
    b, s, _ = qkv.shape
    h = N_HEADS_SB
    blk_q = pl.BlockSpec((1, ATT_TQ, HEAD_DIM), lambda bi, hi, qi: (bi, qi, hi))
    return pl.pallas_call(
        _sb_kernel,
        out_shape=jax.ShapeDtypeStruct((b, s, h * HEAD_DIM), BF16),
        grid=(b, h, s // ATT_TQ),
        in_specs=[blk_q,
                  pl.BlockSpec((1, s, HEAD_DIM), lambda bi, hi, qi: (bi, 0, h + hi)),
                  pl.BlockSpec((1, s, HEAD_DIM), lambda bi, hi, qi: (bi, 0, 2 * h + hi)),
                  pl.BlockSpec((1, HEAD_DIM), lambda bi, hi, qi: (0, hi))],
        out_specs=blk_q,
        compiler_params=_params(3),
        name="sb_attention",
    )(qkv, qkv, qkv, g_out.reshape(1, h * HEAD_DIM))


def _moba_kernel(q_ref, k_ref, v_ref, cos_ref, sin_ref, g_ref, o_ref, kr_ref, km_ref):
    n = pl.program_id(2)
    blk = MOBA_BLOCK
    nb = k_ref.shape[1] // blk

    @pl.when(n == 0)
    def _():
        km_ref[...] = jnp.zeros_like(km_ref)
        for j in range(nb):
            rows = pl.ds(j * blk, blk)
            kr = _rotate(k_ref[0, rows, :].astype(F32), cos_ref[0, rows, :], sin_ref[0, rows, :])
            kr_ref[rows, :] = kr.astype(BF16)
            km_ref[j:j + 1, :] = jnp.mean(kr, axis=0, keepdims=True)

    q_rows = pl.ds(pl.multiple_of(n * blk, blk), blk)
    q = _rotate(q_ref[0].astype(F32), cos_ref[0, q_rows, :], sin_ref[0, q_rows, :]).astype(BF16)

    km = km_ref[...]
    km_hi = km.astype(BF16)
    km_lo = (km - km_hi.astype(F32)).astype(BF16)
    gate = _dot_nt(q, km_hi) + _dot_nt(q, km_lo)
    lane = lax.broadcasted_iota(jnp.int32, gate.shape, 1)
    rank = jnp.zeros(gate.shape, F32)
    for c in range(nb - 1):
        gc = gate[:, c:c + 1]
        beats = (gc > gate) | ((gc == gate) & (c < lane))
        rank = rank + jnp.where(beats, 1.0, 0.0) * (c < n).astype(F32)
    selected = jnp.where((rank < MOBA_TOP_K) & (lane < n), 1.0, 0.0)

    row = lax.broadcasted_iota(jnp.int32, (blk, blk), 0)
    col = lax.broadcasted_iota(jnp.int32, (blk, blk), 1)

    s = _dot_nt(q, kr_ref[q_rows, :]) * ATTN_SCALE
    s = jnp.where(col <= row, s, -jnp.inf)
    m = jnp.max(s, axis=-1, keepdims=True)
    p = jnp.exp(s - m)
    l = jnp.sum(p, axis=-1, keepdims=True)
    acc = _dot(p.astype(BF16), v_ref[0, q_rows, :])

    def body(j, state):
        m, l, acc = state
        rows = pl.ds(pl.multiple_of(j * blk, blk), blk)
        sel_j = jnp.sum(jnp.where(lane == j, selected, 0.0), axis=-1, keepdims=True)
        s = _dot_nt(q, kr_ref[rows, :]) * ATTN_SCALE + jnp.where(sel_j > 0.0, 0.0, NEG_BIG)
        m_new = jnp.maximum(m, jnp.max(s, axis=-1, keepdims=True))
        alpha = jnp.exp(m - m_new)
        p = jnp.exp(s - m_new)
        l = alpha * l + jnp.sum(p, axis=-1, keepdims=True)
        acc = alpha * acc + _dot(p.astype(BF16), v_ref[0, rows, :])
        return m_new, l, acc

    m, l, acc = lax.fori_loop(0, n, body, (m, l, acc))
    o_ref[0] = _rmsnorm_rows(acc / l, g_ref[...]).astype(o_ref.dtype)


def _moba_attention(qkv, cos, sin, g_out):
    b, s, _ = qkv.shape
    h = N_HEADS_MOBA
    base = 3 * N_HEADS_SB
    assert ATT_TQ == MOBA_BLOCK and s % MOBA_BLOCK == 0 and s // MOBA_BLOCK <= HEAD_DIM
    tab = pl.BlockSpec((1, s, HEAD_DIM), lambda bi, hi, qi: (bi, 0, 0))
    return pl.pallas_call(
        _moba_kernel,
        out_shape=jax.ShapeDtypeStruct((b, s, h * HEAD_DIM), BF16),
        grid=(b, h, s // MOBA_BLOCK),
        in_specs=[pl.BlockSpec((1, MOBA_BLOCK, HEAD_DIM), lambda bi, hi, qi: (bi, qi, base + hi)),
                  pl.BlockSpec((1, s, HEAD_DIM), lambda bi, hi, qi: (bi, 0, base + h + hi)),
                  pl.BlockSpec((1, s, HEAD_DIM), lambda bi, hi, qi: (bi, 0, base + 2 * h + hi)),
                  tab, tab,
                  pl.BlockSpec((1, HEAD_DIM), lambda bi, hi, qi: (0, hi))],
        out_specs=pl.BlockSpec((1, MOBA_BLOCK, HEAD_DIM), lambda bi, hi, qi: (bi, qi, hi)),
        scratch_shapes=[pltpu.VMEM((s, HEAD_DIM), BF16),
                        pltpu.VMEM((HEAD_DIM, HEAD_DIM), F32)],
        compiler_params=_params(3),
        name="moba_attention",
    )(qkv, qkv, qkv, cos, sin, g_out.reshape(1, h * HEAD_DIM))


def _xattn_kernel(q_ref, kv_ref, o_ref):
    w = N_HEADS_X * HEAD_DIM
    for hh in range(N_HEADS_X):
        cols = slice(hh * HEAD_DIM, (hh + 1) * HEAD_DIM)
        q = q_ref[0, :, cols]
        k = kv_ref[0, :, cols]
        v = kv_ref[0, :, w + hh * HEAD_DIM: w + (hh + 1) * HEAD_DIM]
        s = _dot_nt(q, k) * ATTN_SCALE
        p = jnp.exp(s - jnp.max(s, axis=-1, keepdims=True))
        l = jnp.sum(p, axis=-1, keepdims=True)
        o_ref[0, :, cols] = (_dot(p.astype(BF16), v) / l).astype(o_ref.dtype)


def _cross_attention(q, kv):
    b, s, w = q.shape
    n_mem = kv.shape[1]
    return pl.pallas_call(
        _xattn_kernel,
        out_shape=jax.ShapeDtypeStruct((b, s, w), BF16),
        grid=(b, s // XATT_TQ),
        in_specs=[pl.BlockSpec((1, XATT_TQ, w), lambda bi, qi: (bi, qi, 0)),
                  pl.BlockSpec((1, n_mem, 2 * w), lambda bi, qi: (bi, 0, 0))],
        out_specs=pl.BlockSpec((1, XATT_TQ, w), lambda bi, qi: (bi, qi, 0)),
        compiler_params=_params(2),
        name="cross_attention",
    )(q, kv)


def kernel(x, mem, positions, g_mix, w_in, g_out_sb, g_out_moba, w_out, g_xattn, g_mem,
           w_xq, w_xkv, w_xo, g_ffn, w_gate_up, w_down, g_final):
    b, s, d = x.shape
    n_mem = mem.shape[1]
    n = b * s
    x0 = x.reshape(n, d)

    h = _rmsnorm(x0, g_mix, BF16)
    qkv = _matmul(h, w_in.astype(BF16), BF16, name="in_proj").reshape(b, s, -1)
    cos, sin = _rope_tables(positions)
    o_sb = _sb_attention(qkv, g_out_sb)
    o_mb = _moba_attention(qkv, cos, sin, g_out_moba)
    x1 = _mix_out(o_sb.reshape(n, -1), o_mb.reshape(n, -1), w_out.astype(BF16), x0)

    h = _rmsnorm(x1, g_xattn, BF16)
    m = _rmsnorm(mem.reshape(b * n_mem, d), g_mem, BF16)
    qx = _matmul(h, w_xq.astype(BF16), BF16, name="xq_proj")
    kv = _matmul(m, w_xkv.astype(BF16), BF16, name="xkv_proj")
    ox = _cross_attention(qx.reshape(b, s, -1), kv.reshape(b, n_mem, -1))
    x2 = _matmul(ox.reshape(n, -1), w_xo.astype(BF16), F32, residual=x1, name="xo_proj")

    h = _rmsnorm(x2, g_ffn, BF16)
    a = _swiglu_up(h, w_gate_up.astype(BF16))
    x3 = _matmul(a, w_down.astype(BF16), F32, residual=x2, tn=FFN_TN, name="down_proj")

    return _rmsnorm(x3, g_final, F32).reshape(b, s, d)
```

```python
import functools

import jax
import jax.numpy as jnp
from jax import lax
from jax.experimental import pallas as pl
from jax.experimental.pallas import tpu as pltpu

F32 = jnp.float32
BF16 = jnp.bfloat16

HEAD_DIM = 128
N_HEADS_SB = 8
N_HEADS_MOBA = 8
N_HEADS_X = 4
MOBA_BLOCK = 256
MOBA_TOP_K = 3
ROPE_DIM = HEAD_DIM // 4
ROPE_THETA = 500000.0
EPS = 1e-6
ATTN_SCALE = HEAD_DIM ** -0.5
LOG2E = 1.4426950408889634

V7X_VMEM_BYTES = 64 * 1024 * 1024
VMEM_LIMIT_BYTES = V7X_VMEM_BYTES * 7 // 8

NORM_ROWS = 512
MM_TM = 1024
MM_TN = 1024
FFN_TN = 512
ATT_TQ = 256
ATT_TK = 256
XATT_TQ = 512
SB_HEADS_PER_STEP = 2
NEG_BIG = -1e30

_NT = (((1,), (1,)), ((), ()))


def _params(n_axes):
    return pltpu.CompilerParams(
        dimension_semantics=("arbitrary",) * n_axes,
        vmem_limit_bytes=VMEM_LIMIT_BYTES)


def _dot(a, b):
    return jnp.dot(a, b, preferred_element_type=F32)


def _dot_nt(a, b):
    return lax.dot_general(a, b, _NT, preferred_element_type=F32)


def _rmsnorm_rows(x, g):
    ms = jnp.mean(x * x, axis=-1, keepdims=True)
    return x * lax.rsqrt(ms + EPS) * g


def _rmsnorm_kernel(x_ref, g_ref, o_ref):
    o_ref[...] = _rmsnorm_rows(x_ref[...].astype(F32), g_ref[...]).astype(o_ref.dtype)


def _rmsnorm(x, g, out_dtype):
    n, d = x.shape
    return pl.pallas_call(
        _rmsnorm_kernel,
        out_shape=jax.ShapeDtypeStruct((n, d), out_dtype),
        grid=(n // NORM_ROWS,),
        in_specs=[pl.BlockSpec((NORM_ROWS, d), lambda i: (i, 0)),
                  pl.BlockSpec((1, d), lambda i: (0, 0))],
        out_specs=pl.BlockSpec((NORM_ROWS, d), lambda i: (i, 0)),
        compiler_params=_params(1),
        name="rmsnorm",
    )(x, g.reshape(1, d))


def _matmul_kernel(a_ref, w_ref, o_ref):
    o_ref[...] = _dot(a_ref[...], w_ref[...]).astype(o_ref.dtype)


def _matmul_res_kernel(a_ref, w_ref, r_ref, o_ref):
    o_ref[...] = (r_ref[...] + _dot(a_ref[...], w_ref[...])).astype(o_ref.dtype)


def _matmul_scaled_kernel(a_ref, w_ref, s_ref, o_ref):
    o_ref[...] = (_dot(a_ref[...], w_ref[...]) * s_ref[...]).astype(o_ref.dtype)


def _matmul(a, w, out_dtype, residual=None, col_scale=None, tm=MM_TM, tn=MM_TN, name="matmul"):
    m, k = a.shape
    n = w.shape[1]
    tm, tn = min(tm, m), min(tn, n)
    in_specs = [pl.BlockSpec((tm, k), lambda i, j: (i, 0)),
                pl.BlockSpec((k, tn), lambda i, j: (0, j))]
    args = [a, w]
    body = _matmul_kernel
    assert residual is None or col_scale is None
    if residual is not None:
        in_specs.append(pl.BlockSpec((tm, tn), lambda i, j: (i, j)))
        args.append(residual)
        body = _matmul_res_kernel
    if col_scale is not None:
        in_specs.append(pl.BlockSpec((1, tn), lambda i, j: (0, j)))
        args.append(col_scale.reshape(1, n))
        body = _matmul_scaled_kernel
    return pl.pallas_call(
        body,
        out_shape=jax.ShapeDtypeStruct((m, n), out_dtype),
        grid=(m // tm, n // tn),
        in_specs=in_specs,
        out_specs=pl.BlockSpec((tm, tn), lambda i, j: (i, j)),
        compiler_params=_params(2),
        name=name,
    )(*args)


def _mix_out_kernel(a1_ref, a2_ref, w1_ref, w2_ref, r_ref, o_ref):
    acc = _dot(a1_ref[...], w1_ref[...]) + _dot(a2_ref[...], w2_ref[...])
    o_ref[...] = r_ref[...] + acc


def _mix_out(o_sb, o_mb, w_out, residual):
    m, k1 = o_sb.shape
    k2 = o_mb.shape[1]
    n = w_out.shape[1]
    assert k1 == k2
    tm, tn = MM_TM, MM_TN
    return pl.pallas_call(
        _mix_out_kernel,
        out_shape=jax.ShapeDtypeStruct((m, n), F32),
        grid=(m // tm, n // tn),
        in_specs=[pl.BlockSpec((tm, k1), lambda i, j: (i, 0)),
                  pl.BlockSpec((tm, k2), lambda i, j: (i, 0)),
                  pl.BlockSpec((k1, tn), lambda i, j: (0, j)),
                  pl.BlockSpec((k2, tn), lambda i, j: (1, j)),
                  pl.BlockSpec((tm, tn), lambda i, j: (i, j))],
        out_specs=pl.BlockSpec((tm, tn), lambda i, j: (i, j)),
        compiler_params=_params(2),
        name="mix_out",
    )(o_sb, o_mb, w_out, w_out, residual)


def _swiglu_kernel(a_ref, wg_ref, wu_ref, o_ref):
    a = a_ref[...]
    g = _dot(a, wg_ref[...])
    u = _dot(a, wu_ref[...])
    o_ref[...] = (g / (1.0 + jnp.exp(-g)) * u).astype(o_ref.dtype)


def _swiglu_up(h, w_gate_up):
    m, k = h.shape
    d_ff = w_gate_up.shape[1] // 2
    tm, tn = MM_TM, FFN_TN
    nj = d_ff // tn
    return pl.pallas_call(
        _swiglu_kernel,
        out_shape=jax.ShapeDtypeStruct((m, d_ff), BF16),
        grid=(m // tm, nj),
        in_specs=[pl.BlockSpec((tm, k), lambda i, j: (i, 0)),
                  pl.BlockSpec((k, tn), lambda i, j: (0, j)),
                  pl.BlockSpec((k, tn), lambda i, j: (0, j + nj))],
        out_specs=pl.BlockSpec((tm, tn), lambda i, j: (i, j)),
        compiler_params=_params(2),
        name="swiglu_up",
    )(h, w_gate_up, w_gate_up)


def _rope_table_kernel(pos_ref, invf_ref, sign_ref, cos_ref, sin_ref):
    ang = pos_ref[0].astype(F32) * invf_ref[...]
    cos_ref[0] = jnp.cos(ang)
    sin_ref[0] = jnp.sin(ang) * sign_ref[...]


def _rope_tables(positions):
    b, s = positions.shape
    half = ROPE_DIM // 2
    inv_freq = ROPE_THETA ** (-jnp.arange(0, ROPE_DIM, 2, dtype=F32) / ROPE_DIM)
    pad = jnp.zeros((HEAD_DIM - ROPE_DIM,), F32)
    invf = jnp.concatenate([inv_freq, inv_freq, pad]).reshape(1, HEAD_DIM)
    sign = jnp.concatenate([-jnp.ones((half,), F32), jnp.ones((half,), F32), pad]).reshape(1, HEAD_DIM)
    tab = jax.ShapeDtypeStruct((b, s, HEAD_DIM), F32)
    return pl.pallas_call(
        _rope_table_kernel,
        out_shape=(tab, tab),
        grid=(b,),
        in_specs=[pl.BlockSpec((1, s, 1), lambda i: (i, 0, 0)),
                  pl.BlockSpec((1, HEAD_DIM), lambda i: (0, 0)),
                  pl.BlockSpec((1, HEAD_DIM), lambda i: (0, 0))],
        out_specs=(pl.BlockSpec((1, s, HEAD_DIM), lambda i: (i, 0, 0)),
                   pl.BlockSpec((1, s, HEAD_DIM), lambda i: (i, 0, 0))),
        compiler_params=_params(1),
        name="rope_tables",
    )(positions.reshape(b, s, 1), invf, sign)


def _rotate(x, cos, sin_signed):
    half = ROPE_DIM // 2
    lane = lax.broadcasted_iota(jnp.int32, x.shape, 1)
    partner = jnp.where(lane < half,
                        pltpu.roll(x, HEAD_DIM - half, axis=1),
                        pltpu.roll(x, half, axis=1))
    return x * cos + partner * sin_signed


def _sb_kernel(q_ref, k_ref, v_ref, g_ref, o_ref):
    t = ATT_TQ
    n_tiles = k_ref.shape[1] // t
    row = lax.broadcasted_iota(jnp.int32, (t, t), 0)
    col = lax.broadcasted_iota(jnp.int32, (t, t), 1)
    past = col < row
    neg_from = jnp.where(row >= col, -1.0, 0.0).astype(BF16)
    g = g_ref[...]

    for i in range(n_tiles):
        keys = (i + 1) * t
        q = q_ref[0, i * t:(i + 1) * t, :]
        z = _dot_nt(q, k_ref[0, :keys, :])
        sp = jnp.maximum(z, 0.0) + jnp.log(1.0 + jnp.exp(-jnp.abs(z)))
        carry = jnp.zeros((t, 1), F32)
        w = [None] * (i + 1)
        for j in reversed(range(i + 1)):
            cols = slice(j * t, (j + 1) * t)
            sp_j = sp[:, cols]
            if j == i:
                sp_j = jnp.where(past, sp_j, 0.0)
            tail = _dot(sp_j.astype(BF16), neg_from)
            w_j = jnp.exp(z[:, cols] + tail + carry)
            if j == i:
                w_j = jnp.where(past, w_j, 0.0)
            w[j] = w_j.astype(BF16)
            carry = carry + tail[:, 0:1]
        acc = _dot(jnp.concatenate(w, axis=1), v_ref[0, :keys, :])
        o_ref[0, i * t:(i + 1) * t, :] = _rmsnorm_rows(acc, g).astype(o_ref.dtype)


def _sb_attention(qkv, g_out):
    b, s, _ = qkv.shape
    h = N_HEADS_SB
    blk = lambda base: pl.BlockSpec((1, s, HEAD_DIM), lambda bi, hi: (bi, 0, base + hi))
    return pl.pallas_call(
        _sb_kernel,
        out_shape=jax.ShapeDtypeStruct((b, s, h * HEAD_DIM), BF16),
        grid=(b, h),
        in_specs=[blk(0), blk(h), blk(2 * h),
                  pl.BlockSpec((1, HEAD_DIM), lambda bi, hi: (0, hi))],
        out_specs=blk(0),
        compiler_params=_params(2),
        name="sb_attention",
    )(qkv, qkv, qkv, g_out.reshape(1, h * HEAD_DIM))


def _moba_kernel(q_ref, k_ref, v_ref, cos_ref, sin_ref, g_ref, o_ref, ka_ref, km_ref):
    blk = MOBA_BLOCK
    nb = k_ref.shape[1] // blk
    d = HEAD_DIM
    row = lax.broadcasted_iota(jnp.int32, (blk, blk), 0)
    col = lax.broadcasted_iota(jnp.int32, (blk, blk), 1)
    causal = col <= row
    lane = lax.broadcasted_iota(jnp.int32, (blk, d), 1)
    g = g_ref[...]

    km_ref[...] = jnp.zeros_like(km_ref)
    for j in range(nb):
        rows = slice(j * blk, (j + 1) * blk)
        kr = _rotate(k_ref[0, rows, :].astype(F32), cos_ref[0, rows, :], sin_ref[0, rows, :])
        ka_ref[rows, :d] = kr.astype(BF16)
        ka_ref[rows, d:] = jnp.where(lane == j, 1.0, 0.0).astype(BF16)
        km_ref[j:j + 1, :] = jnp.mean(kr, axis=0, keepdims=True)

    km = km_ref[...]
    km_hi = km.astype(BF16)
    km_lo = (km - km_hi.astype(F32)).astype(BF16)
    blk_id = lax.broadcasted_iota(jnp.int32, (km.shape[0], blk), 0)

    for n in range(nb):
        rows = slice(n * blk, (n + 1) * blk)
        keys = (n + 1) * blk
        q = _rotate(q_ref[0, rows, :].astype(F32), cos_ref[0, rows, :], sin_ref[0, rows, :]).astype(BF16)
        if n <= MOBA_TOP_K:
            s = _dot_nt(q, ka_ref[:keys, :d])
        else:
            gate = _dot_nt(km_hi, q) + _dot_nt(km_lo, q)
            rank = jnp.zeros(gate.shape, F32)
            for c in range(n):
                gc = gate[c:c + 1, :]
                beats = (gc > gate) | ((gc == gate) & (c < blk_id))
                rank = rank + jnp.where(beats, 1.0, 0.0)
            allowed = ((rank < MOBA_TOP_K) & (blk_id < n)) | (blk_id == n)
            bias_t = jnp.where(allowed, 0.0, NEG_BIG)
            bias_t = jnp.concatenate([bias_t, jnp.zeros((d - bias_t.shape[0], blk), F32)], axis=0)
            q_ext = jnp.concatenate([q, bias_t.T.astype(BF16)], axis=1)
            s = _dot_nt(q_ext, ka_ref[:keys, :])
        s_own = jnp.where(causal, s[:, n * blk:], -jnp.inf)
        m = jnp.max(s_own, axis=-1, keepdims=True)
        if n > 0:
            s_past = s[:, :n * blk]
            m = jnp.maximum(m, jnp.max(s_past, axis=-1, keepdims=True))
            p = jnp.concatenate([jnp.exp(s_past - m), jnp.exp(s_own - m)], axis=1)
        else:
            p = jnp.exp(s_own - m)
        l = jnp.sum(p, axis=-1, keepdims=True)
        acc = _dot(p.astype(BF16), v_ref[0, :keys, :])
        o_ref[0, rows, :] = _rmsnorm_rows(acc / l, g).astype(o_ref.dtype)


def _moba_attention(qkv, cos, sin, g_out):
    b, s, _ = qkv.shape
    h = N_HEADS_MOBA
    base = 3 * N_HEADS_SB
    km_rows = 16
    assert s % MOBA_BLOCK == 0 and s // MOBA_BLOCK <= km_rows
    blk = lambda off: pl.BlockSpec((1, s, HEAD_DIM), lambda bi, hi: (bi, 0, off + hi))
    tab = pl.BlockSpec((1, s, HEAD_DIM), lambda bi, hi: (bi, 0, 0))
    return pl.pallas_call(
        _moba_kernel,
        out_shape=jax.ShapeDtypeStruct((b, s, h * HEAD_DIM), BF16),
        grid=(b, h),
        in_specs=[blk(base), blk(base + h), blk(base + 2 * h), tab, tab,
                  pl.BlockSpec((1, HEAD_DIM), lambda bi, hi: (0, hi))],
        out_specs=blk(0),
        scratch_shapes=[pltpu.VMEM((s, 2 * HEAD_DIM), BF16),
                        pltpu.VMEM((km_rows, HEAD_DIM), F32)],
        compiler_params=_params(2),
        name="moba_attention",
    )(qkv, qkv, qkv, cos, sin, g_out.reshape(1, h * HEAD_DIM))


def _xattn_kernel(q_ref, kv_ref, o_ref):
    w = N_HEADS_X * HEAD_DIM
    for hh in range(N_HEADS_X):
        cols = slice(hh * HEAD_DIM, (hh + 1) * HEAD_DIM)
        q = q_ref[0, :, cols]
        k = kv_ref[0, :, cols]
        v = kv_ref[0, :, w + hh * HEAD_DIM: w + (hh + 1) * HEAD_DIM]
        s = _dot_nt(q, k) * ATTN_SCALE
        p = jnp.exp(s - jnp.max(s, axis=-1, keepdims=True))
        l = jnp.sum(p, axis=-1, keepdims=True)
        o_ref[0, :, cols] = (_dot(p.astype(BF16), v) / l).astype(o_ref.dtype)


def _cross_attention(q, kv):
    b, s, w = q.shape
    n_mem = kv.shape[1]
    return pl.pallas_call(
        _xattn_kernel,
        out_shape=jax.ShapeDtypeStruct((b, s, w), BF16),
        grid=(b, s // XATT_TQ),
        in_specs=[pl.BlockSpec((1, XATT_TQ, w), lambda bi, qi: (bi, qi, 0)),
                  pl.BlockSpec((1, n_mem, 2 * w), lambda bi, qi: (bi, 0, 0))],
        out_specs=pl.BlockSpec((1, XATT_TQ, w), lambda bi, qi: (bi, qi, 0)),
        compiler_params=_params(2),
        name="cross_attention",
    )(q, kv)


def kernel(x, mem, positions, g_mix, w_in, g_out_sb, g_out_moba, w_out, g_xattn, g_mem,
           w_xq, w_xkv, w_xo, g_ffn, w_gate_up, w_down, g_final):
    b, s, d = x.shape
    n_mem = mem.shape[1]
    n = b * s
    x0 = x.reshape(n, d)

    h = _rmsnorm(x0, g_mix, BF16)
    w_sb, w_mb = N_HEADS_SB * HEAD_DIM, N_HEADS_MOBA * HEAD_DIM
    q_scale = jnp.concatenate([jnp.full((w_sb,), ATTN_SCALE, F32), jnp.ones((2 * w_sb,), F32),
                               jnp.full((w_mb,), ATTN_SCALE, F32), jnp.ones((2 * w_mb,), F32)])
    qkv = _matmul(h, w_in.astype(BF16), BF16, col_scale=q_scale, name="in_proj").reshape(b, s, -1)
    cos, sin = _rope_tables(positions)
    o_sb = _sb_attention(qkv, g_out_sb)
    o_mb = _moba_attention(qkv, cos, sin, g_out_moba)
    x1 = _mix_out(o_sb.reshape(n, -1), o_mb.reshape(n, -1), w_out.astype(BF16), x0)

    h = _rmsnorm(x1, g_xattn, BF16)
    m = _rmsnorm(mem.reshape(b * n_mem, d), g_mem, BF16)
    qx = _matmul(h, w_xq.astype(BF16), BF16, name="xq_proj")
    kv = _matmul(m, w_xkv.astype(BF16), BF16, name="xkv_proj")
    ox = _cross_attention(qx.reshape(b, s, -1), kv.reshape(b, n_mem, -1))
    x2 = _matmul(ox.reshape(n, -1), w_xo.astype(BF16), F32, residual=x1, name="xo_proj")

    h = _rmsnorm(x2, g_ffn, BF16)
    a = _swiglu_up(h, w_gate_up.astype(BF16))
    x3 = _matmul(a, w_down.astype(BF16), F32, residual=x2, tn=FFN_TN, name="down_proj")

    return _rmsnorm(x3, g_final, F32).reshape(b, s, d)
```

```python
import functools

import jax
import jax.numpy as jnp
from jax import lax
from jax.experimental import pallas as pl
from jax.experimental.pallas import tpu as pltpu

F32 = jnp.float32
BF16 = jnp.bfloat16

HEAD_DIM = 128
N_HEADS_SB = 8
N_HEADS_MOBA = 8
N_HEADS_X = 4
MOBA_BLOCK = 256
MOBA_TOP_K = 3
ROPE_DIM = HEAD_DIM // 4
ROPE_THETA = 500000.0
EPS = 1e-6
ATTN_SCALE = HEAD_DIM ** -0.5

V7X_VMEM_BYTES = 64 * 1024 * 1024
VMEM_LIMIT_BYTES = V7X_VMEM_BYTES * 7 // 8

NORM_ROWS = 512
MM_TM = 1024
MM_TN = 1024
FFN_TN = 512
FUSED_TM = 512
ATT_TQ = 256
NEG_BIG = -1e30

_NT = (((1,), (1,)), ((), ()))


def _params(n_axes):
    return pltpu.CompilerParams(
        dimension_semantics=("arbitrary",) * n_axes,
        vmem_limit_bytes=VMEM_LIMIT_BYTES)


def _dot(a, b):
    return jnp.dot(a, b, preferred_element_type=F32)


def _dot_nt(a, b):
    return lax.dot_general(a, b, _NT, preferred_element_type=F32)


def _rmsnorm_rows(x, g):
    ms = jnp.mean(x * x, axis=-1, keepdims=True)
    return x * lax.rsqrt(ms + EPS) * g


def _rmsnorm_kernel(x_ref, g_ref, o_ref):
    o_ref[...] = _rmsnorm_rows(x_ref[...].astype(F32), g_ref[...]).astype(o_ref.dtype)


def _rmsnorm(x, g, out_dtype):
    n, d = x.shape
    return pl.pallas_call(
        _rmsnorm_kernel,
        out_shape=jax.ShapeDtypeStruct((n, d), out_dtype),
        grid=(n // NORM_ROWS,),
        in_specs=[pl.BlockSpec((NORM_ROWS, d), lambda i: (i, 0)),
                  pl.BlockSpec((1, d), lambda i: (0, 0))],
        out_specs=pl.BlockSpec((NORM_ROWS, d), lambda i: (i, 0)),
        compiler_params=_params(1),
        name="rmsnorm",
    )(x, g.reshape(1, d))


def _matmul_kernel(a_ref, w_ref, o_ref):
    o_ref[...] = _dot(a_ref[...], w_ref[...]).astype(o_ref.dtype)


def _matmul(a, w, out_dtype, tm=MM_TM, tn=MM_TN, name="matmul"):
    m, k = a.shape
    n = w.shape[1]
    tm, tn = min(tm, m), min(tn, n)
    return pl.pallas_call(
        _matmul_kernel,
        out_shape=jax.ShapeDtypeStruct((m, n), out_dtype),
        grid=(m // tm, n // tn),
        in_specs=[pl.BlockSpec((tm, k), lambda i, j: (i, 0)),
                  pl.BlockSpec((k, tn), lambda i, j: (0, j))],
        out_specs=pl.BlockSpec((tm, tn), lambda i, j: (i, j)),
        compiler_params=_params(2),
        name=name,
    )(a, w)


def _in_proj_kernel(x_ref, g_ref, w_ref, s_ref, o_ref, h_ref):
    @pl.when(pl.program_id(1) == 0)
    def _():
        h_ref[...] = _rmsnorm_rows(x_ref[...], g_ref[...]).astype(h_ref.dtype)

    o_ref[...] = (_dot(h_ref[...], w_ref[...]) * s_ref[...]).astype(o_ref.dtype)


def _in_proj(x, g, w, col_scale):
    m, k = x.shape
    n = w.shape[1]
    tm, tn = MM_TM, MM_TN
    return pl.pallas_call(
        _in_proj_kernel,
        out_shape=jax.ShapeDtypeStruct((m, n), BF16),
        grid=(m // tm, n // tn),
        in_specs=[pl.BlockSpec((tm, k), lambda i, j: (i, 0)),
                  pl.BlockSpec((1, k), lambda i, j: (0, 0)),
                  pl.BlockSpec((k, tn), lambda i, j: (0, j)),
                  pl.BlockSpec((1, tn), lambda i, j: (0, j))],
        out_specs=pl.BlockSpec((tm, tn), lambda i, j: (i, j)),
        scratch_shapes=[pltpu.VMEM((tm, k), BF16)],
        compiler_params=_params(2),
        name="in_proj",
    )(x, g.reshape(1, k), w, col_scale.reshape(1, n))


def _mixer_xattn_kernel(osb_ref, omb_ref, wout_ref, x_ref, gx_ref, wq_ref, kv_ref, wo_ref, gf_ref,
                        x2_ref, h3_ref):
    k1 = osb_ref.shape[1]
    x1 = x_ref[...] + _dot(osb_ref[...], wout_ref[:k1, :]) + _dot(omb_ref[...], wout_ref[k1:, :])
    h2 = _rmsnorm_rows(x1, gx_ref[...]).astype(BF16)
    q = (_dot(h2, wq_ref[...]) * ATTN_SCALE).astype(BF16)
    w = wq_ref.shape[1]
    heads = []
    for hh in range(w // HEAD_DIM):
        cols = slice(hh * HEAD_DIM, (hh + 1) * HEAD_DIM)
        s = _dot_nt(q[:, cols], kv_ref[0, :, cols])
        p = jnp.exp(s - jnp.max(s, axis=-1, keepdims=True))
        l = jnp.sum(p, axis=-1, keepdims=True)
        v = kv_ref[0, :, w + hh * HEAD_DIM: w + (hh + 1) * HEAD_DIM]
        heads.append((_dot(p.astype(BF16), v) / l).astype(BF16))
    x2 = x1 + _dot(jnp.concatenate(heads, axis=1), wo_ref[...])
    x2_ref[...] = x2
    h3_ref[...] = _rmsnorm_rows(x2, gf_ref[...]).astype(h3_ref.dtype)


def _mixer_out_xattn(o_sb, o_mb, w_out, x, g_xattn, w_xq, kv, w_xo, g_ffn, rows_per_batch):
    m, d = x.shape
    k1 = o_sb.shape[1]
    wx = w_xq.shape[1]
    n_mem = kv.shape[1]
    tm = FUSED_TM
    assert rows_per_batch % tm == 0
    tiles_per_batch = rows_per_batch // tm
    const = lambda shape: pl.BlockSpec(shape, lambda i: (0,) * len(shape), pipeline_mode=pl.Buffered(1))
    rows = lambda width: pl.BlockSpec((tm, width), lambda i: (i, 0))
    return pl.pallas_call(
        _mixer_xattn_kernel,
        out_shape=(jax.ShapeDtypeStruct((m, d), F32), jax.ShapeDtypeStruct((m, d), BF16)),
        grid=(m // tm,),
        in_specs=[rows(k1), rows(k1), const((2 * k1, d)), rows(d), const((1, d)), const((d, wx)),
                  pl.BlockSpec((1, n_mem, 2 * wx), lambda i: (i // tiles_per_batch, 0, 0)),
                  const((wx, d)), const((1, d))],
        out_specs=(rows(d), rows(d)),
        compiler_params=_params(1),
        name="mixer_out_xattn",
    )(o_sb, o_mb, w_out, x, g_xattn.reshape(1, d), w_xq, kv, w_xo, g_ffn.reshape(1, d))


def _down_final_kernel(a_ref, w_ref, r_ref, g_ref, o_ref, acc_ref):
    kk = pl.program_id(1)
    last = pl.num_programs(1) - 1

    @pl.when(kk == 0)
    def _():
        acc_ref[...] = _dot(a_ref[...], w_ref[...])

    @pl.when((kk > 0) & (kk < last))
    def _():
        acc_ref[...] += _dot(a_ref[...], w_ref[...])

    @pl.when(kk == last)
    def _():
        x3 = r_ref[...] + (acc_ref[...] + _dot(a_ref[...], w_ref[...]))
        o_ref[...] = _rmsnorm_rows(x3, g_ref[...])


def _down_final(a, w, residual, g):
    m, k = a.shape
    n = w.shape[1]
    tm, tk = MM_TM, FFN_TN
    return pl.pallas_call(
        _down_final_kernel,
        out_shape=jax.ShapeDtypeStruct((m, n), F32),
        grid=(m // tm, k // tk),
        in_specs=[pl.BlockSpec((tm, tk), lambda i, kk: (i, kk)),
                  pl.BlockSpec((tk, n), lambda i, kk: (kk, 0)),
                  pl.BlockSpec((tm, n), lambda i, kk: (i, 0)),
                  pl.BlockSpec((1, n), lambda i, kk: (0, 0))],
        out_specs=pl.BlockSpec((tm, n), lambda i, kk: (i, 0)),
        scratch_shapes=[pltpu.VMEM((tm, n), F32)],
        compiler_params=_params(2),
        name="down_final",
    )(a, w, residual, g.reshape(1, n))


def _swiglu_kernel(a_ref, wg_ref, wu_ref, o_ref):
    a = a_ref[...]
    g = _dot(a, wg_ref[...])
    u = _dot(a, wu_ref[...])
    o_ref[...] = (g / (1.0 + jnp.exp(-g)) * u).astype(o_ref.dtype)


def _swiglu_up(h, w_gate_up):
    m, k = h.shape
    d_ff = w_gate_up.shape[1] // 2
    tm, tn = MM_TM, FFN_TN
    nj = d_ff // tn
    return pl.pallas_call(
        _swiglu_kernel,
        out_shape=jax.ShapeDtypeStruct((m, d_ff), BF16),
        grid=(m // tm, nj),
        in_specs=[pl.BlockSpec((tm, k), lambda i, j: (i, 0)),
                  pl.BlockSpec((k, tn), lambda i, j: (0, j)),
                  pl.BlockSpec((k, tn), lambda i, j: (0, j + nj))],
        out_specs=pl.BlockSpec((tm, tn), lambda i, j: (i, j)),
        compiler_params=_params(2),
        name="swiglu_up",
    )(h, w_gate_up, w_gate_up)


def _rope_table_kernel(pos_ref, invf_ref, sign_ref, cos_ref, sin_ref):
    ang = pos_ref[0].astype(F32) * invf_ref[...]
    cos_ref[0] = jnp.cos(ang)
    sin_ref[0] = jnp.sin(ang) * sign_ref[...]


def _rope_tables(positions):
    b, s = positions.shape
    half = ROPE_DIM // 2
    inv_freq = ROPE_THETA ** (-jnp.arange(0, ROPE_DIM, 2, dtype=F32) / ROPE_DIM)
    pad = jnp.zeros((HEAD_DIM - ROPE_DIM,), F32)
    invf = jnp.concatenate([inv_freq, inv_freq, pad]).reshape(1, HEAD_DIM)
    sign = jnp.concatenate([-jnp.ones((half,), F32), jnp.ones((half,), F32), pad]).reshape(1, HEAD_DIM)
    tab = jax.ShapeDtypeStruct((b, s, HEAD_DIM), F32)
    return pl.pallas_call(
        _rope_table_kernel,
        out_shape=(tab, tab),
        grid=(b,),
        in_specs=[pl.BlockSpec((1, s, 1), lambda i: (i, 0, 0)),
                  pl.BlockSpec((1, HEAD_DIM), lambda i: (0, 0)),
                  pl.BlockSpec((1, HEAD_DIM), lambda i: (0, 0))],
        out_specs=(pl.BlockSpec((1, s, HEAD_DIM), lambda i: (i, 0, 0)),
                   pl.BlockSpec((1, s, HEAD_DIM), lambda i: (i, 0, 0))),
        compiler_params=_params(1),
        name="rope_tables",
    )(positions.reshape(b, s, 1), invf, sign)


def _rotate(x, cos, sin_signed):
    half = ROPE_DIM // 2
    lane = lax.broadcasted_iota(jnp.int32, x.shape, 1)
    partner = jnp.where(lane < half,
                        pltpu.roll(x, HEAD_DIM - half, axis=1),
                        pltpu.roll(x, half, axis=1))
    return x * cos + partner * sin_signed


def _sb_kernel(q_ref, k_ref, v_ref, g_ref, o_ref):
    t = ATT_TQ
    n_tiles = k_ref.shape[1] // t
    row = lax.broadcasted_iota(jnp.int32, (t, t), 0)
    col = lax.broadcasted_iota(jnp.int32, (t, t), 1)
    past = col < row
    neg_from = jnp.where(row >= col, -1.0, 0.0).astype(BF16)
    g = g_ref[...]

    for i in range(n_tiles):
        keys = (i + 1) * t
        q = q_ref[0, i * t:(i + 1) * t, :]
        z = _dot_nt(q, k_ref[0, :keys, :])
        sp = jnp.maximum(z, 0.0) + jnp.log(1.0 + jnp.exp(-jnp.abs(z)))
        carry = jnp.zeros((t, 1), F32)
        w = [None] * (i + 1)
        for j in reversed(range(i + 1)):
            cols = slice(j * t, (j + 1) * t)
            sp_j = sp[:, cols]
            if j == i:
                sp_j = jnp.where(past, sp_j, 0.0)
            tail = _dot(sp_j.astype(BF16), neg_from)
            w_j = jnp.exp(z[:, cols] + tail + carry)
            if j == i:
                w_j = jnp.where(past, w_j, 0.0)
            w[j] = w_j.astype(BF16)
            carry = carry + tail[:, 0:1]
        acc = _dot(jnp.concatenate(w, axis=1), v_ref[0, :keys, :])
        o_ref[0, i * t:(i + 1) * t, :] = _rmsnorm_rows(acc, g).astype(o_ref.dtype)


def _sb_attention(qkv, g_out):
    b, s, _ = qkv.shape
    h = N_HEADS_SB
    blk = lambda base: pl.BlockSpec((1, s, HEAD_DIM), lambda bi, hi: (bi, 0, base + hi))
    return pl.pallas_call(
        _sb_kernel,
        out_shape=jax.ShapeDtypeStruct((b, s, h * HEAD_DIM), BF16),
        grid=(b, h),
        in_specs=[blk(0), blk(h), blk(2 * h),
                  pl.BlockSpec((1, HEAD_DIM), lambda bi, hi: (0, hi))],
        out_specs=blk(0),
        compiler_params=_params(2),
        name="sb_attention",
    )(qkv, qkv, qkv, g_out.reshape(1, h * HEAD_DIM))


def _moba_kernel(q_ref, k_ref, v_ref, cos_ref, sin_ref, g_ref, o_ref, ka_ref, km_ref):
    blk = MOBA_BLOCK
    nb = k_ref.shape[1] // blk
    d = HEAD_DIM
    row = lax.broadcasted_iota(jnp.int32, (blk, blk), 0)
    col = lax.broadcasted_iota(jnp.int32, (blk, blk), 1)
    causal = col <= row
    lane = lax.broadcasted_iota(jnp.int32, (blk, d), 1)
    g = g_ref[...]

    km_ref[...] = jnp.zeros_like(km_ref)
    for j in range(nb):
        rows = slice(j * blk, (j + 1) * blk)
        kr = _rotate(k_ref[0, rows, :].astype(F32), cos_ref[0, rows, :], sin_ref[0, rows, :])
        ka_ref[rows, :d] = kr.astype(BF16)
        ka_ref[rows, d:] = jnp.where(lane == j, 1.0, 0.0).astype(BF16)
        km_ref[j:j + 1, :] = jnp.mean(kr, axis=0, keepdims=True)

    km = km_ref[...]
    km_hi = km.astype(BF16)
    km_lo = (km - km_hi.astype(F32)).astype(BF16)
    blk_id = lax.broadcasted_iota(jnp.int32, (km.shape[0], blk), 0)

    for n in range(nb):
        rows = slice(n * blk, (n + 1) * blk)
        keys = (n + 1) * blk
        q = _rotate(q_ref[0, rows, :].astype(F32), cos_ref[0, rows, :], sin_ref[0, rows, :]).astype(BF16)
        if n <= MOBA_TOP_K:
            s = _dot_nt(q, ka_ref[:keys, :d])
        else:
            gate = _dot_nt(km_hi, q) + _dot_nt(km_lo, q)
            rank = jnp.zeros(gate.shape, F32)
            for c in range(n):
                gc = gate[c:c + 1, :]
                beats = (gc > gate) | ((gc == gate) & (c < blk_id))
                rank = rank + jnp.where(beats, 1.0, 0.0)
            allowed = ((rank < MOBA_TOP_K) & (blk_id < n)) | (blk_id == n)
            bias_t = jnp.where(allowed, 0.0, NEG_BIG)
            bias_t = jnp.concatenate([bias_t, jnp.zeros((d - bias_t.shape[0], blk), F32)], axis=0)
            q_ext = jnp.concatenate([q, bias_t.T.astype(BF16)], axis=1)
            s = _dot_nt(q_ext, ka_ref[:keys, :])
        s_own = jnp.where(causal, s[:, n * blk:], -jnp.inf)
        m = jnp.max(s_own, axis=-1, keepdims=True)
        if n > 0:
            s_past = s[:, :n * blk]
            m = jnp.maximum(m, jnp.max(s_past, axis=-1, keepdims=True))
            p = jnp.concatenate([jnp.exp(s_past - m), jnp.exp(s_own - m)], axis=1)
        else:
            p = jnp.exp(s_own - m)
        l = jnp.sum(p, axis=-1, keepdims=True)
        acc = _dot(p.astype(BF16), v_ref[0, :keys, :])
        o_ref[0, rows, :] = _rmsnorm_rows(acc / l, g).astype(o_ref.dtype)


def _moba_attention(qkv, cos, sin, g_out):
    b, s, _ = qkv.shape
    h = N_HEADS_MOBA
    base = 3 * N_HEADS_SB
    km_rows = 16
    assert s % MOBA_BLOCK == 0 and s // MOBA_BLOCK <= km_rows
    blk = lambda off: pl.BlockSpec((1, s, HEAD_DIM), lambda bi, hi: (bi, 0, off + hi))
    tab = pl.BlockSpec((1, s, HEAD_DIM), lambda bi, hi: (bi, 0, 0))
    return pl.pallas_call(
        _moba_kernel,
        out_shape=jax.ShapeDtypeStruct((b, s, h * HEAD_DIM), BF16),
        grid=(b, h),
        in_specs=[blk(base), blk(base + h), blk(base + 2 * h), tab, tab,
                  pl.BlockSpec((1, HEAD_DIM), lambda bi, hi: (0, hi))],
        out_specs=blk(0),
        scratch_shapes=[pltpu.VMEM((s, 2 * HEAD_DIM), BF16),
                        pltpu.VMEM((km_rows, HEAD_DIM), F32)],
        compiler_params=_params(2),
        name="moba_attention",
    )(qkv, qkv, qkv, cos, sin, g_out.reshape(1, h * HEAD_DIM))


def kernel(x, mem, positions, g_mix, w_in, g_out_sb, g_out_moba, w_out, g_xattn, g_mem,
           w_xq, w_xkv, w_xo, g_ffn, w_gate_up, w_down, g_final):
    b, s, d = x.shape
    n_mem = mem.shape[1]
    n = b * s
    x0 = x.reshape(n, d)

    w_sb, w_mb = N_HEADS_SB * HEAD_DIM, N_HEADS_MOBA * HEAD_DIM
    q_scale = jnp.concatenate([jnp.full((w_sb,), ATTN_SCALE, F32), jnp.ones((2 * w_sb,), F32),
                               jnp.full((w_mb,), ATTN_SCALE, F32), jnp.ones((2 * w_mb,), F32)])
    qkv = _in_proj(x0, g_mix, w_in.astype(BF16), q_scale).reshape(b, s, -1)
    cos, sin = _rope_tables(positions)
    o_sb = _sb_attention(qkv, g_out_sb)
    o_mb = _moba_attention(qkv, cos, sin, g_out_moba)

    m = _rmsnorm(mem.reshape(b * n_mem, d), g_mem, BF16)
    kv = _matmul(m, w_xkv.astype(BF16), BF16, name="xkv_proj").reshape(b, n_mem, -1)

    x2, h = _mixer_out_xattn(o_sb.reshape(n, -1), o_mb.reshape(n, -1), w_out.astype(BF16), x0,
                             g_xattn, w_xq.astype(BF16), kv, w_xo.astype(BF16), g_ffn, s)

    a = _swiglu_up(h, w_gate_up.astype(BF16))
    return _down_final(a, w_down.astype(BF16), x2, g_final).reshape(b, s, d)
```

```python
import functools

import jax
import jax.numpy as jnp
from jax import lax
from jax.experimental import pallas as pl
from jax.experimental.pallas import tpu as pltpu

F32 = jnp.float32
BF16 = jnp.bfloat16

HEAD_DIM = 128
N_HEADS_SB = 8
N_HEADS_MOBA = 8
N_HEADS_X = 4
MOBA_BLOCK = 256
MOBA_TOP_K = 3
ROPE_DIM = HEAD_DIM // 4
ROPE_THETA = 500000.0
EPS = 1e-6
ATTN_SCALE = HEAD_DIM ** -0.5

V7X_VMEM_BYTES = 64 * 1024 * 1024
VMEM_LIMIT_BYTES = V7X_VMEM_BYTES * 7 // 8

NORM_ROWS = 512
MM_TM = 1024
MM_TN = 1024
FFN_TN = 512
FUSED_TM = 512
DOWN_TM = 256
ATT_TQ = 256
MOBA_HEADS_PER_STEP = 1
NEG_BIG = -1e30

_NT = (((1,), (1,)), ((), ()))


def _params(n_axes):
    return pltpu.CompilerParams(
        dimension_semantics=("arbitrary",) * n_axes,
        vmem_limit_bytes=VMEM_LIMIT_BYTES)


def _dot(a, b):
    return jnp.dot(a, b, preferred_element_type=F32)


def _dot_nt(a, b):
    return lax.dot_general(a, b, _NT, preferred_element_type=F32)


def _rmsnorm_rows(x, g):
    ms = jnp.mean(x * x, axis=-1, keepdims=True)
    return x * lax.rsqrt(ms + EPS) * g


def _rmsnorm_kernel(x_ref, g_ref, o_ref):
    o_ref[...] = _rmsnorm_rows(x_ref[...].astype(F32), g_ref[...]).astype(o_ref.dtype)


def _rmsnorm(x, g, out_dtype):
    n, d = x.shape
    return pl.pallas_call(
        _rmsnorm_kernel,
        out_shape=jax.ShapeDtypeStruct((n, d), out_dtype),
        grid=(n // NORM_ROWS,),
        in_specs=[pl.BlockSpec((NORM_ROWS, d), lambda i: (i, 0)),
                  pl.BlockSpec((1, d), lambda i: (0, 0))],
        out_specs=pl.BlockSpec((NORM_ROWS, d), lambda i: (i, 0)),
        compiler_params=_params(1),
        name="rmsnorm",
    )(x, g.reshape(1, d))


def _matmul_kernel(a_ref, w_ref, o_ref):
    o_ref[...] = _dot(a_ref[...], w_ref[...]).astype(o_ref.dtype)


def _matmul(a, w, out_dtype, tm=MM_TM, tn=MM_TN, name="matmul"):
    m, k = a.shape
    n = w.shape[1]
    tm, tn = min(tm, m), min(tn, n)
    return pl.pallas_call(
        _matmul_kernel,
        out_shape=jax.ShapeDtypeStruct((m, n), out_dtype),
        grid=(m // tm, n // tn),
        in_specs=[pl.BlockSpec((tm, k), lambda i, j: (i, 0)),
                  pl.BlockSpec((k, tn), lambda i, j: (0, j))],
        out_specs=pl.BlockSpec((tm, tn), lambda i, j: (i, j)),
        compiler_params=_params(2),
        name=name,
    )(a, w)


def _in_proj_kernel(x_ref, g_ref, w_ref, s_ref, o_ref, h_ref):
    @pl.when(pl.program_id(1) == 0)
    def _():
        h_ref[...] = _rmsnorm_rows(x_ref[...], g_ref[...]).astype(h_ref.dtype)

    o_ref[...] = (_dot(h_ref[...], w_ref[...]) * s_ref[...]).astype(o_ref.dtype)


def _in_proj(x, g, w, col_scale):
    m, k = x.shape
    n = w.shape[1]
    tm, tn = MM_TM, MM_TN
    return pl.pallas_call(
        _in_proj_kernel,
        out_shape=jax.ShapeDtypeStruct((m, n), BF16),
        grid=(m // tm, n // tn),
        in_specs=[pl.BlockSpec((tm, k), lambda i, j: (i, 0)),
                  pl.BlockSpec((1, k), lambda i, j: (0, 0)),
                  pl.BlockSpec((k, tn), lambda i, j: (0, j)),
                  pl.BlockSpec((1, tn), lambda i, j: (0, j))],
        out_specs=pl.BlockSpec((tm, tn), lambda i, j: (i, j)),
        scratch_shapes=[pltpu.VMEM((tm, k), BF16)],
        compiler_params=_params(2),
        name="in_proj",
    )(x, g.reshape(1, k), w, col_scale.reshape(1, n))


def _mixer_xattn_kernel(osb_ref, omb_ref, wout_ref, x_ref, gx_ref, wq_ref, kv_ref, wo_ref, gf_ref,
                        x2_ref, h3_ref):
    k1 = osb_ref.shape[1]
    x1 = x_ref[...] + _dot(osb_ref[...], wout_ref[:k1, :]) + _dot(omb_ref[...], wout_ref[k1:, :])
    h2 = _rmsnorm_rows(x1, gx_ref[...]).astype(BF16)
    q = (_dot(h2, wq_ref[...]) * ATTN_SCALE).astype(BF16)
    w = wq_ref.shape[1]
    heads = []
    for hh in range(w // HEAD_DIM):
        cols = slice(hh * HEAD_DIM, (hh + 1) * HEAD_DIM)
        s = _dot_nt(q[:, cols], kv_ref[0, :, cols])
        p = jnp.exp(s - jnp.max(s, axis=-1, keepdims=True))
        l = jnp.sum(p, axis=-1, keepdims=True)
        v = kv_ref[0, :, w + hh * HEAD_DIM: w + (hh + 1) * HEAD_DIM]
        heads.append((_dot(p.astype(BF16), v) / l).astype(BF16))
    x2 = x1 + _dot(jnp.concatenate(heads, axis=1), wo_ref[...])
    x2_ref[...] = x2
    h3_ref[...] = _rmsnorm_rows(x2, gf_ref[...]).astype(h3_ref.dtype)


def _mixer_out_xattn(o_sb, o_mb, w_out, x, g_xattn, w_xq, kv, w_xo, g_ffn, rows_per_batch):
    m, d = x.shape
    k1 = o_sb.shape[1]
    wx = w_xq.shape[1]
    n_mem = kv.shape[1]
    tm = FUSED_TM
    assert rows_per_batch % tm == 0
    tiles_per_batch = rows_per_batch // tm
    const = lambda shape: pl.BlockSpec(shape, lambda i: (0,) * len(shape), pipeline_mode=pl.Buffered(1))
    rows = lambda width: pl.BlockSpec((tm, width), lambda i: (i, 0))
    return pl.pallas_call(
        _mixer_xattn_kernel,
        out_shape=(jax.ShapeDtypeStruct((m, d), F32), jax.ShapeDtypeStruct((m, d), BF16)),
        grid=(m // tm,),
        in_specs=[rows(k1), rows(k1), const((2 * k1, d)), rows(d), const((1, d)), const((d, wx)),
                  pl.BlockSpec((1, n_mem, 2 * wx), lambda i: (i // tiles_per_batch, 0, 0)),
                  const((wx, d)), const((1, d))],
        out_specs=(rows(d), rows(d)),
        compiler_params=_params(1),
        name="mixer_out_xattn",
    )(o_sb, o_mb, w_out, x, g_xattn.reshape(1, d), w_xq, kv, w_xo, g_ffn.reshape(1, d))


def _down_final_kernel(a_ref, w_ref, r_ref, g_ref, o_ref):
    x3 = r_ref[...] + _dot(a_ref[...], w_ref[...])
    o_ref[...] = _rmsnorm_rows(x3, g_ref[...])


def _down_final(a, w, residual, g):
    m, k = a.shape
    n = w.shape[1]
    tm = DOWN_TM
    return pl.pallas_call(
        _down_final_kernel,
        out_shape=jax.ShapeDtypeStruct((m, n), F32),
        grid=(m // tm,),
        in_specs=[pl.BlockSpec((tm, k), lambda i: (i, 0)),
                  pl.BlockSpec((k, n), lambda i: (0, 0), pipeline_mode=pl.Buffered(1)),
                  pl.BlockSpec((tm, n), lambda i: (i, 0)),
                  pl.BlockSpec((1, n), lambda i: (0, 0))],
        out_specs=pl.BlockSpec((tm, n), lambda i: (i, 0)),
        compiler_params=_params(1),
        name="down_final",
    )(a, w, residual, g.reshape(1, n))


def _swiglu_kernel(a_ref, wg_ref, wu_ref, o_ref, wg_bf, wu_bf):
    @pl.when(pl.program_id(1) == 0)
    def _():
        wg_bf[...] = wg_ref[...].astype(BF16)
        wu_bf[...] = wu_ref[...].astype(BF16)

    a = a_ref[...]
    g = _dot(a, wg_bf[...])
    u = _dot(a, wu_bf[...])
    o_ref[...] = (g / (1.0 + jnp.exp(-g)) * u).astype(o_ref.dtype)


def _swiglu_up(h, w_gate_up):
    m, k = h.shape
    d_ff = w_gate_up.shape[1] // 2
    tm, tn = MM_TM, FFN_TN
    nj = d_ff // tn
    return pl.pallas_call(
        _swiglu_kernel,
        out_shape=jax.ShapeDtypeStruct((m, d_ff), BF16),
        grid=(nj, m // tm),
        in_specs=[pl.BlockSpec((tm, k), lambda j, i: (i, 0)),
                  pl.BlockSpec((k, tn), lambda j, i: (0, j)),
                  pl.BlockSpec((k, tn), lambda j, i: (0, j + nj))],
        out_specs=pl.BlockSpec((tm, tn), lambda j, i: (i, j)),
        scratch_shapes=[pltpu.VMEM((k, tn), BF16), pltpu.VMEM((k, tn), BF16)],
        compiler_params=_params(2),
        name="swiglu_up",
    )(h, w_gate_up, w_gate_up)


def _rope_table_kernel(pos_ref, invf_ref, sign_ref, cos_ref, sin_ref):
    ang = pos_ref[0].astype(F32) * invf_ref[...]
    cos_ref[0] = jnp.cos(ang)
    sin_ref[0] = jnp.sin(ang) * sign_ref[...]


def _rope_tables(positions):
    b, s = positions.shape
    half = ROPE_DIM // 2
    inv_freq = ROPE_THETA ** (-jnp.arange(0, ROPE_DIM, 2, dtype=F32) / ROPE_DIM)
    pad = jnp.zeros((HEAD_DIM - ROPE_DIM,), F32)
    invf = jnp.concatenate([inv_freq, inv_freq, pad]).reshape(1, HEAD_DIM)
    sign = jnp.concatenate([-jnp.ones((half,), F32), jnp.ones((half,), F32), pad]).reshape(1, HEAD_DIM)
    tab = jax.ShapeDtypeStruct((b, s, HEAD_DIM), F32)
    return pl.pallas_call(
        _rope_table_kernel,
        out_shape=(tab, tab),
        grid=(b,),
        in_specs=[pl.BlockSpec((1, s, 1), lambda i: (i, 0, 0)),
                  pl.BlockSpec((1, HEAD_DIM), lambda i: (0, 0)),
                  pl.BlockSpec((1, HEAD_DIM), lambda i: (0, 0))],
        out_specs=(pl.BlockSpec((1, s, HEAD_DIM), lambda i: (i, 0, 0)),
                   pl.BlockSpec((1, s, HEAD_DIM), lambda i: (i, 0, 0))),
        compiler_params=_params(1),
        name="rope_tables",
    )(positions.reshape(b, s, 1), invf, sign)


def _rotate(x, cos, sin_signed):
    half = ROPE_DIM // 2
    lane = lax.broadcasted_iota(jnp.int32, x.shape, 1)
    partner = jnp.where(lane < half,
                        pltpu.roll(x, HEAD_DIM - half, axis=1),
                        pltpu.roll(x, half, axis=1))
    return x * cos + partner * sin_signed


def _sb_kernel(q_ref, k_ref, v_ref, g_ref, o_ref):
    t = ATT_TQ
    n_tiles = k_ref.shape[1] // t
    row = lax.broadcasted_iota(jnp.int32, (t, t), 0)
    col = lax.broadcasted_iota(jnp.int32, (t, t), 1)
    past = col < row
    neg_from = jnp.where(row >= col, -1.0, 0.0).astype(BF16)
    g = g_ref[...]

    for i in range(n_tiles):
        keys = (i + 1) * t
        q = q_ref[0, i * t:(i + 1) * t, :]
        z = _dot_nt(q, k_ref[0, :keys, :])
        sp = jnp.maximum(z, 0.0) + jnp.log(1.0 + jnp.exp(-jnp.abs(z)))
        carry = jnp.zeros((t, 1), F32)
        w = [None] * (i + 1)
        for j in reversed(range(i + 1)):
            cols = slice(j * t, (j + 1) * t)
            sp_j = sp[:, cols]
            if j == i:
                sp_j = jnp.where(past, sp_j, 0.0)
            tail = _dot(sp_j.astype(BF16), neg_from)
            w_j = jnp.exp(z[:, cols] + tail + carry)
            if j == i:
                w_j = jnp.where(past, w_j, 0.0)
            w[j] = w_j.astype(BF16)
            carry = carry + tail[:, 0:1]
        acc = _dot(jnp.concatenate(w, axis=1), v_ref[0, :keys, :])
        o_ref[0, i * t:(i + 1) * t, :] = _rmsnorm_rows(acc, g).astype(o_ref.dtype)


def _sb_attention(qkv, g_out):
    b, s, _ = qkv.shape
    h = N_HEADS_SB
    blk = lambda base: pl.BlockSpec((1, s, HEAD_DIM), lambda bi, hi: (bi, 0, base + hi))
    return pl.pallas_call(
        _sb_kernel,
        out_shape=jax.ShapeDtypeStruct((b, s, h * HEAD_DIM), BF16),
        grid=(b, h),
        in_specs=[blk(0), blk(h), blk(2 * h),
                  pl.BlockSpec((1, HEAD_DIM), lambda bi, hi: (0, hi))],
        out_specs=blk(0),
        compiler_params=_params(2),
        name="sb_attention",
    )(qkv, qkv, qkv, g_out.reshape(1, h * HEAD_DIM))


def _moba_kernel(q_ref, k_ref, v_ref, cos_ref, sin_ref, g_ref, o_ref, ka_ref, km_ref):
    blk = MOBA_BLOCK
    nb = k_ref.shape[1] // blk
    d = HEAD_DIM
    heads = k_ref.shape[2] // d
    row = lax.broadcasted_iota(jnp.int32, (blk, blk), 0)
    col = lax.broadcasted_iota(jnp.int32, (blk, blk), 1)
    causal = col <= row
    lane = lax.broadcasted_iota(jnp.int32, (blk, d), 1)
    blk_id = lax.broadcasted_iota(jnp.int32, (km_ref.shape[1], blk), 0)

    km_ref[...] = jnp.zeros_like(km_ref)
    for j in range(nb):
        rows = slice(j * blk, (j + 1) * blk)
        for hh in range(heads):
            cols = slice(hh * d, (hh + 1) * d)
            kr = _rotate(k_ref[0, rows, cols].astype(F32), cos_ref[0, rows, :], sin_ref[0, rows, :])
            ka_ref[hh, rows, :d] = kr.astype(BF16)
            ka_ref[hh, rows, d:] = jnp.where(lane == j, 1.0, 0.0).astype(BF16)
            km_ref[hh, j:j + 1, :] = jnp.mean(kr, axis=0, keepdims=True)

    def scores(hh, n):
        cols = slice(hh * d, (hh + 1) * d)
        rows = slice(n * blk, (n + 1) * blk)
        keys = (n + 1) * blk
        q = _rotate(q_ref[0, rows, cols].astype(F32), cos_ref[0, rows, :], sin_ref[0, rows, :]).astype(BF16)
        if n <= MOBA_TOP_K:
            s = _dot_nt(q, ka_ref[hh, :keys, :d])
        else:
            km = km_ref[hh]
            km_hi = km.astype(BF16)
            km_lo = (km - km_hi.astype(F32)).astype(BF16)
            gate = _dot_nt(km_hi, q) + _dot_nt(km_lo, q)
            rank = jnp.zeros(gate.shape, F32)
            for c in range(n):
                gc = gate[c:c + 1, :]
                beats = (gc > gate) | ((gc == gate) & (c < blk_id))
                rank = rank + jnp.where(beats, 1.0, 0.0)
            allowed = ((rank < MOBA_TOP_K) & (blk_id < n)) | (blk_id == n)
            bias_t = jnp.where(allowed, 0.0, NEG_BIG)
            bias_t = jnp.concatenate([bias_t, jnp.zeros((d - bias_t.shape[0], blk), F32)], axis=0)
            q_ext = jnp.concatenate([q, bias_t.T.astype(BF16)], axis=1)
            s = _dot_nt(q_ext, ka_ref[hh, :keys, :])
        return s

    def attend(hh, n, s):
        cols = slice(hh * d, (hh + 1) * d)
        rows = slice(n * blk, (n + 1) * blk)
        keys = (n + 1) * blk
        s_own = jnp.where(causal, s[:, n * blk:], -jnp.inf)
        m = jnp.max(s_own, axis=-1, keepdims=True)
        if n > 0:
            s_past = s[:, :n * blk]
            m = jnp.maximum(m, jnp.max(s_past, axis=-1, keepdims=True))
            p = jnp.concatenate([jnp.exp(s_past - m), jnp.exp(s_own - m)], axis=1)
        else:
            p = jnp.exp(s_own - m)
        l = jnp.sum(p, axis=-1, keepdims=True)
        acc = _dot(p.astype(BF16), v_ref[0, :keys, cols])
        o_ref[0, rows, cols] = _rmsnorm_rows(acc / l, g_ref[:, cols]).astype(o_ref.dtype)

    items = [(hh, n) for n in range(nb) for hh in range(heads)]
    pending = None
    for item in items:
        s = scores(*item)
        if pending is not None:
            attend(*pending)
        pending = (*item, s)
    attend(*pending)


def _moba_attention(qkv, cos, sin, g_out):
    b, s, _ = qkv.shape
    h = N_HEADS_MOBA
    hps = MOBA_HEADS_PER_STEP
    groups = h // hps
    width = hps * HEAD_DIM
    base = 3 * N_HEADS_SB // hps
    km_rows = 16
    assert s % MOBA_BLOCK == 0 and s // MOBA_BLOCK <= km_rows
    blk = lambda off: pl.BlockSpec((1, s, width), lambda bi, gi: (bi, 0, off + gi))
    tab = pl.BlockSpec((1, s, HEAD_DIM), lambda bi, gi: (bi, 0, 0))
    return pl.pallas_call(
        _moba_kernel,
        out_shape=jax.ShapeDtypeStruct((b, s, h * HEAD_DIM), BF16),
        grid=(b, groups),
        in_specs=[blk(base), blk(base + groups), blk(base + 2 * groups), tab, tab,
                  pl.BlockSpec((1, width), lambda bi, gi: (0, gi))],
        out_specs=blk(0),
        scratch_shapes=[pltpu.VMEM((hps, s, 2 * HEAD_DIM), BF16),
                        pltpu.VMEM((hps, km_rows, HEAD_DIM), F32)],
        compiler_params=_params(2),
        name="moba_attention",
    )(qkv, qkv, qkv, cos, sin, g_out.reshape(1, h * HEAD_DIM))


def kernel(x, mem, positions, g_mix, w_in, g_out_sb, g_out_moba, w_out, g_xattn, g_mem,
           w_xq, w_xkv, w_xo, g_ffn, w_gate_up, w_down, g_final):
    b, s, d = x.shape
    n_mem = mem.shape[1]
    n = b * s
    x0 = x.reshape(n, d)

    w_sb, w_mb = N_HEADS_SB * HEAD_DIM, N_HEADS_MOBA * HEAD_DIM
    q_scale = jnp.concatenate([jnp.full((w_sb,), ATTN_SCALE, F32), jnp.ones((2 * w_sb,), F32),
                               jnp.full((w_mb,), ATTN_SCALE, F32), jnp.ones((2 * w_mb,), F32)])
    qkv = _in_proj(x0, g_mix, w_in.astype(BF16), q_scale).reshape(b, s, -1)
    cos, sin = _rope_tables(positions)
    o_sb = _sb_attention(qkv, g_out_sb)
    o_mb = _moba_attention(qkv, cos, sin, g_out_moba)

    m = _rmsnorm(mem.reshape(b * n_mem, d), g_mem, BF16)
    kv = _matmul(m, w_xkv.astype(BF16), BF16, name="xkv_proj").reshape(b, n_mem, -1)

    x2, h = _mixer_out_xattn(o_sb.reshape(n, -1), o_mb.reshape(n, -1), w_out.astype(BF16), x0,
                             g_xattn, w_xq.astype(BF16), kv, w_xo.astype(BF16), g_ffn, s)

    a = _swiglu_up(h, w_gate_up)
    return _down_final(a, w_down.astype(BF16), x2, g_final).reshape(b, s, d)
```

```python
import functools

import jax
import jax.numpy as jnp
from jax import lax
from jax.experimental import pallas as pl
from jax.experimental.pallas import tpu as pltpu

F32 = jnp.float32
BF16 = jnp.bfloat16

HEAD_DIM = 128
N_HEADS_SB = 8
N_HEADS_MOBA = 8
N_HEADS_X = 4
MOBA_BLOCK = 256
MOBA_TOP_K = 3
ROPE_DIM = HEAD_DIM // 4
ROPE_THETA = 500000.0
EPS = 1e-6
ATTN_SCALE = HEAD_DIM ** -0.5

V7X_VMEM_BYTES = 64 * 1024 * 1024
VMEM_LIMIT_BYTES = V7X_VMEM_BYTES * 7 // 8

NORM_ROWS = 512
MM_TM = 1024
MM_TN = 1024
FFN_TN = 512
FUSED_TM = 512
DOWN_TM = 256
ATT_TQ = 256
MOBA_HEADS_PER_STEP = 1
NEG_BIG = -1e30

_NT = (((1,), (1,)), ((), ()))


def _params(n_axes):
    return pltpu.CompilerParams(
        dimension_semantics=("arbitrary",) * n_axes,
        vmem_limit_bytes=VMEM_LIMIT_BYTES)


def _dot(a, b):
    return jnp.dot(a, b, preferred_element_type=F32)


def _dot_nt(a, b):
    return lax.dot_general(a, b, _NT, preferred_element_type=F32)


def _rmsnorm_rows(x, g):
    ms = jnp.mean(x * x, axis=-1, keepdims=True)
    return x * lax.rsqrt(ms + EPS) * g


def _rmsnorm_kernel(x_ref, g_ref, o_ref):
    o_ref[...] = _rmsnorm_rows(x_ref[...].astype(F32), g_ref[...]).astype(o_ref.dtype)


def _rmsnorm(x, g, out_dtype):
    n, d = x.shape
    return pl.pallas_call(
        _rmsnorm_kernel,
        out_shape=jax.ShapeDtypeStruct((n, d), out_dtype),
        grid=(n // NORM_ROWS,),
        in_specs=[pl.BlockSpec((NORM_ROWS, d), lambda i: (i, 0)),
                  pl.BlockSpec((1, d), lambda i: (0, 0))],
        out_specs=pl.BlockSpec((NORM_ROWS, d), lambda i: (i, 0)),
        compiler_params=_params(1),
        name="rmsnorm",
    )(x, g.reshape(1, d))


def _matmul_kernel(a_ref, w_ref, o_ref):
    o_ref[...] = _dot(a_ref[...], w_ref[...]).astype(o_ref.dtype)


def _matmul(a, w, out_dtype, tm=MM_TM, tn=MM_TN, name="matmul"):
    m, k = a.shape
    n = w.shape[1]
    tm, tn = min(tm, m), min(tn, n)
    return pl.pallas_call(
        _matmul_kernel,
        out_shape=jax.ShapeDtypeStruct((m, n), out_dtype),
        grid=(m // tm, n // tn),
        in_specs=[pl.BlockSpec((tm, k), lambda i, j: (i, 0)),
                  pl.BlockSpec((k, tn), lambda i, j: (0, j))],
        out_specs=pl.BlockSpec((tm, tn), lambda i, j: (i, j)),
        compiler_params=_params(2),
        name=name,
    )(a, w)


def _in_proj_kernel(x_ref, g_ref, w_ref, s_ref, o_ref, h_ref):
    @pl.when(pl.program_id(1) == 0)
    def _():
        h_ref[...] = _rmsnorm_rows(x_ref[...], g_ref[...]).astype(h_ref.dtype)

    o_ref[...] = (_dot(h_ref[...], w_ref[...]) * s_ref[...]).astype(o_ref.dtype)


def _in_proj(x, g, w, col_scale):
    m, k = x.shape
    n = w.shape[1]
    tm, tn = MM_TM, MM_TN
    return pl.pallas_call(
        _in_proj_kernel,
        out_shape=jax.ShapeDtypeStruct((m, n), BF16),
        grid=(m // tm, n // tn),
        in_specs=[pl.BlockSpec((tm, k), lambda i, j: (i, 0)),
                  pl.BlockSpec((1, k), lambda i, j: (0, 0)),
                  pl.BlockSpec((k, tn), lambda i, j: (0, j)),
                  pl.BlockSpec((1, tn), lambda i, j: (0, j))],
        out_specs=pl.BlockSpec((tm, tn), lambda i, j: (i, j)),
        scratch_shapes=[pltpu.VMEM((tm, k), BF16)],
        compiler_params=_params(2),
        name="in_proj",
    )(x, g.reshape(1, k), w, col_scale.reshape(1, n))


def _mixer_xattn_kernel(osb_ref, omb_ref, wout_ref, x_ref, gx_ref, wq_ref, kv_ref, wo_ref, gf_ref,
                        x2_ref, h3_ref):
    k1 = osb_ref.shape[1]
    x1 = x_ref[...] + _dot(osb_ref[...], wout_ref[:k1, :]) + _dot(omb_ref[...], wout_ref[k1:, :])
    h2 = _rmsnorm_rows(x1, gx_ref[...]).astype(BF16)
    q = (_dot(h2, wq_ref[...]) * ATTN_SCALE).astype(BF16)
    w = wq_ref.shape[1]
    heads = []
    for hh in range(w // HEAD_DIM):
        cols = slice(hh * HEAD_DIM, (hh + 1) * HEAD_DIM)
        s = _dot_nt(q[:, cols], kv_ref[0, :, cols])
        p = jnp.exp(s - jnp.max(s, axis=-1, keepdims=True))
        l = jnp.sum(p, axis=-1, keepdims=True)
        v = kv_ref[0, :, w + hh * HEAD_DIM: w + (hh + 1) * HEAD_DIM]
        heads.append((_dot(p.astype(BF16), v) / l).astype(BF16))
    x2 = x1 + _dot(jnp.concatenate(heads, axis=1), wo_ref[...])
    x2_ref[...] = x2
    h3_ref[...] = _rmsnorm_rows(x2, gf_ref[...]).astype(h3_ref.dtype)


def _mixer_out_xattn(o_sb, o_mb, w_out, x, g_xattn, w_xq, kv, w_xo, g_ffn, rows_per_batch):
    m, d = x.shape
    k1 = o_sb.shape[1]
    wx = w_xq.shape[1]
    n_mem = kv.shape[1]
    tm = FUSED_TM
    assert rows_per_batch % tm == 0
    tiles_per_batch = rows_per_batch // tm
    const = lambda shape: pl.BlockSpec(shape, lambda i: (0,) * len(shape), pipeline_mode=pl.Buffered(1))
    rows = lambda width: pl.BlockSpec((tm, width), lambda i: (i, 0))
    return pl.pallas_call(
        _mixer_xattn_kernel,
        out_shape=(jax.ShapeDtypeStruct((m, d), F32), jax.ShapeDtypeStruct((m, d), BF16)),
        grid=(m // tm,),
        in_specs=[rows(k1), rows(k1), const((2 * k1, d)), rows(d), const((1, d)), const((d, wx)),
                  pl.BlockSpec((1, n_mem, 2 * wx), lambda i: (i // tiles_per_batch, 0, 0)),
                  const((wx, d)), const((1, d))],
        out_specs=(rows(d), rows(d)),
        compiler_params=_params(1),
        name="mixer_out_xattn",
    )(o_sb, o_mb, w_out, x, g_xattn.reshape(1, d), w_xq, kv, w_xo, g_ffn.reshape(1, d))


def _down_final_kernel(a_ref, w_ref, r_ref, g_ref, o_ref):
    x3 = r_ref[...] + _dot(a_ref[...], w_ref[...])
    o_ref[...] = _rmsnorm_rows(x3, g_ref[...])


def _down_final(a, w, residual, g):
    m, k = a.shape
    n = w.shape[1]
    tm = DOWN_TM
    return pl.pallas_call(
        _down_final_kernel,
        out_shape=jax.ShapeDtypeStruct((m, n), F32),
        grid=(m // tm,),
        in_specs=[pl.BlockSpec((tm, k), lambda i: (i, 0)),
                  pl.BlockSpec((k, n), lambda i: (0, 0), pipeline_mode=pl.Buffered(1)),
                  pl.BlockSpec((tm, n), lambda i: (i, 0)),
                  pl.BlockSpec((1, n), lambda i: (0, 0))],
        out_specs=pl.BlockSpec((tm, n), lambda i: (i, 0)),
        compiler_params=_params(1),
        name="down_final",
    )(a, w, residual, g.reshape(1, n))


def _swiglu_kernel(a_ref, wg_ref, wu_ref, o_ref, wg_bf, wu_bf):
    @pl.when(pl.program_id(1) == 0)
    def _():
        wg_bf[...] = wg_ref[...].astype(BF16)
        wu_bf[...] = wu_ref[...].astype(BF16)

    a = a_ref[...]
    g = _dot(a, wg_bf[...])
    u = _dot(a, wu_bf[...])
    o_ref[...] = (g / (1.0 + jnp.exp(-g)) * u).astype(o_ref.dtype)


def _swiglu_up(h, w_gate_up):
    m, k = h.shape
    d_ff = w_gate_up.shape[1] // 2
    tm, tn = MM_TM, FFN_TN
    nj = d_ff // tn
    return pl.pallas_call(
        _swiglu_kernel,
        out_shape=jax.ShapeDtypeStruct((m, d_ff), BF16),
        grid=(nj, m // tm),
        in_specs=[pl.BlockSpec((tm, k), lambda j, i: (i, 0)),
                  pl.BlockSpec((k, tn), lambda j, i: (0, j)),
                  pl.BlockSpec((k, tn), lambda j, i: (0, j + nj))],
        out_specs=pl.BlockSpec((tm, tn), lambda j, i: (i, j)),
        scratch_shapes=[pltpu.VMEM((k, tn), BF16), pltpu.VMEM((k, tn), BF16)],
        compiler_params=_params(2),
        name="swiglu_up",
    )(h, w_gate_up, w_gate_up)


def _rope_table_kernel(pos_ref, invf_ref, sign_ref, cos_ref, sin_ref):
    ang = pos_ref[0].astype(F32) * invf_ref[...]
    cos_ref[0] = jnp.cos(ang)
    sin_ref[0] = jnp.sin(ang) * sign_ref[...]


def _rope_tables(positions):
    b, s = positions.shape
    half = ROPE_DIM // 2
    inv_freq = ROPE_THETA ** (-jnp.arange(0, ROPE_DIM, 2, dtype=F32) / ROPE_DIM)
    pad = jnp.zeros((HEAD_DIM - ROPE_DIM,), F32)
    invf = jnp.concatenate([inv_freq, inv_freq, pad]).reshape(1, HEAD_DIM)
    sign = jnp.concatenate([-jnp.ones((half,), F32), jnp.ones((half,), F32), pad]).reshape(1, HEAD_DIM)
    tab = jax.ShapeDtypeStruct((b, s, HEAD_DIM), F32)
    return pl.pallas_call(
        _rope_table_kernel,
        out_shape=(tab, tab),
        grid=(b,),
        in_specs=[pl.BlockSpec((1, s, 1), lambda i: (i, 0, 0)),
                  pl.BlockSpec((1, HEAD_DIM), lambda i: (0, 0)),
                  pl.BlockSpec((1, HEAD_DIM), lambda i: (0, 0))],
        out_specs=(pl.BlockSpec((1, s, HEAD_DIM), lambda i: (i, 0, 0)),
                   pl.BlockSpec((1, s, HEAD_DIM), lambda i: (i, 0, 0))),
        compiler_params=_params(1),
        name="rope_tables",
    )(positions.reshape(b, s, 1), invf, sign)


def _rotate(x, cos, sin_signed):
    half = ROPE_DIM // 2
    lane = lax.broadcasted_iota(jnp.int32, x.shape, 1)
    partner = jnp.where(lane < half,
                        pltpu.roll(x, HEAD_DIM - half, axis=1),
                        pltpu.roll(x, half, axis=1))
    return x * cos + partner * sin_signed


def _sb_kernel(q_ref, k_ref, v_ref, g_ref, o_ref, kt_ref):
    t = ATT_TQ
    n_tiles = k_ref.shape[1] // t
    row = lax.broadcasted_iota(jnp.int32, (t, t), 0)
    col = lax.broadcasted_iota(jnp.int32, (t, t), 1)
    past = col < row
    neg_from = jnp.where(row >= col, -1.0, 0.0).astype(BF16)
    g = g_ref[...]

    for j in range(n_tiles):
        kt_ref[:, j * t:(j + 1) * t] = k_ref[0, j * t:(j + 1) * t, :].astype(F32).T.astype(BF16)

    def logits(i):
        return _dot(q_ref[0, i * t:(i + 1) * t, :], kt_ref[:, :(i + 1) * t])

    z_next = logits(0)
    for i in range(n_tiles):
        keys = (i + 1) * t
        z = z_next
        if i + 1 < n_tiles:
            z_next = logits(i + 1)
        sp = jnp.maximum(z, 0.0) + jnp.log(1.0 + jnp.exp(-jnp.abs(z)))
        carry = jnp.zeros((t, 1), F32)
        w = [None] * (i + 1)
        for j in reversed(range(i + 1)):
            cols = slice(j * t, (j + 1) * t)
            sp_j = sp[:, cols]
            if j == i:
                sp_j = jnp.where(past, sp_j, 0.0)
            tail = _dot(sp_j.astype(BF16), neg_from)
            w_j = jnp.exp(z[:, cols] + tail + carry)
            if j == i:
                w_j = jnp.where(past, w_j, 0.0)
            w[j] = w_j.astype(BF16)
            carry = carry + tail[:, 0:1]
        acc = _dot(jnp.concatenate(w, axis=1), v_ref[0, :keys, :])
        o_ref[0, i * t:(i + 1) * t, :] = _rmsnorm_rows(acc, g).astype(o_ref.dtype)


def _sb_attention(qkv, g_out):
    b, s, _ = qkv.shape
    h = N_HEADS_SB
    blk = lambda base: pl.BlockSpec((1, s, HEAD_DIM), lambda bi, hi: (bi, 0, base + hi))
    return pl.pallas_call(
        _sb_kernel,
        out_shape=jax.ShapeDtypeStruct((b, s, h * HEAD_DIM), BF16),
        grid=(b, h),
        in_specs=[blk(0), blk(h), blk(2 * h),
                  pl.BlockSpec((1, HEAD_DIM), lambda bi, hi: (0, hi))],
        out_specs=blk(0),
        scratch_shapes=[pltpu.VMEM((HEAD_DIM, s), BF16)],
        compiler_params=_params(2),
        name="sb_attention",
    )(qkv, qkv, qkv, g_out.reshape(1, h * HEAD_DIM))


def _moba_kernel(q_ref, k_ref, v_ref, cos_ref, sin_ref, g_ref, o_ref, ka_ref, km_ref):
    blk = MOBA_BLOCK
    nb = k_ref.shape[1] // blk
    d = HEAD_DIM
    heads = k_ref.shape[2] // d
    row = lax.broadcasted_iota(jnp.int32, (blk, blk), 0)
    col = lax.broadcasted_iota(jnp.int32, (blk, blk), 1)
    causal = col <= row
    sub = lax.broadcasted_iota(jnp.int32, (d, blk), 0)
    blk_id = lax.broadcasted_iota(jnp.int32, (km_ref.shape[1], blk), 0)

    km_ref[...] = jnp.zeros_like(km_ref)
    for j in range(nb):
        rows = slice(j * blk, (j + 1) * blk)
        for hh in range(heads):
            cols = slice(hh * d, (hh + 1) * d)
            kr = _rotate(k_ref[0, rows, cols].astype(F32), cos_ref[0, rows, :], sin_ref[0, rows, :])
            ka_ref[hh, :d, rows] = kr.T.astype(BF16)
            ka_ref[hh, d:, rows] = jnp.where(sub == j, 1.0, 0.0).astype(BF16)
            km_ref[hh, j:j + 1, :] = jnp.mean(kr, axis=0, keepdims=True)

    def scores(hh, n):
        cols = slice(hh * d, (hh + 1) * d)
        rows = slice(n * blk, (n + 1) * blk)
        keys = (n + 1) * blk
        q = _rotate(q_ref[0, rows, cols].astype(F32), cos_ref[0, rows, :], sin_ref[0, rows, :]).astype(BF16)
        if n <= MOBA_TOP_K:
            s = _dot(q, ka_ref[hh, :d, :keys])
        else:
            km = km_ref[hh]
            km_hi = km.astype(BF16)
            km_lo = (km - km_hi.astype(F32)).astype(BF16)
            gate = _dot_nt(km_hi, q) + _dot_nt(km_lo, q)
            rank = jnp.zeros(gate.shape, F32)
            for c in range(n):
                gc = gate[c:c + 1, :]
                beats = (gc > gate) | ((gc == gate) & (c < blk_id))
                rank = rank + jnp.where(beats, 1.0, 0.0)
            allowed = ((rank < MOBA_TOP_K) & (blk_id < n)) | (blk_id == n)
            bias_t = jnp.where(allowed, 0.0, NEG_BIG)
            bias_t = jnp.concatenate([bias_t, jnp.zeros((d - bias_t.shape[0], blk), F32)], axis=0)
            q_ext = jnp.concatenate([q, bias_t.T.astype(BF16)], axis=1)
            s = _dot(q_ext, ka_ref[hh, :, :keys])
        return s

    def attend(hh, n, s):
        cols = slice(hh * d, (hh + 1) * d)
        rows = slice(n * blk, (n + 1) * blk)
        keys = (n + 1) * blk
        s_own = jnp.where(causal, s[:, n * blk:], -jnp.inf)
        m = jnp.max(s_own, axis=-1, keepdims=True)
        if n > 0:
            s_past = s[:, :n * blk]
            m = jnp.maximum(m, jnp.max(s_past, axis=-1, keepdims=True))
            p = jnp.concatenate([jnp.exp(s_past - m), jnp.exp(s_own - m)], axis=1)
        else:
            p = jnp.exp(s_own - m)
        l = jnp.sum(p, axis=-1, keepdims=True)
        acc = _dot(p.astype(BF16), v_ref[0, :keys, cols])
        o_ref[0, rows, cols] = _rmsnorm_rows(acc / l, g_ref[:, cols]).astype(o_ref.dtype)

    items = [(hh, n) for n in range(nb) for hh in range(heads)]
    pending = None
    for item in items:
        s = scores(*item)
        if pending is not None:
            attend(*pending)
        pending = (*item, s)
    attend(*pending)


def _moba_attention(qkv, cos, sin, g_out):
    b, s, _ = qkv.shape
    h = N_HEADS_MOBA
    hps = MOBA_HEADS_PER_STEP
    groups = h // hps
    width = hps * HEAD_DIM
    base = 3 * N_HEADS_SB // hps
    km_rows = 16
    assert s % MOBA_BLOCK == 0 and s // MOBA_BLOCK <= km_rows
    blk = lambda off: pl.BlockSpec((1, s, width), lambda bi, gi: (bi, 0, off + gi))
    tab = pl.BlockSpec((1, s, HEAD_DIM), lambda bi, gi: (bi, 0, 0))
    return pl.pallas_call(
        _moba_kernel,
        out_shape=jax.ShapeDtypeStruct((b, s, h * HEAD_DIM), BF16),
        grid=(b, groups),
        in_specs=[blk(base), blk(base + groups), blk(base + 2 * groups), tab, tab,
                  pl.BlockSpec((1, width), lambda bi, gi: (0, gi))],
        out_specs=blk(0),
        scratch_shapes=[pltpu.VMEM((hps, 2 * HEAD_DIM, s), BF16),
                        pltpu.VMEM((hps, km_rows, HEAD_DIM), F32)],
        compiler_params=_params(2),
        name="moba_attention",
    )(qkv, qkv, qkv, cos, sin, g_out.reshape(1, h * HEAD_DIM))


def kernel(x, mem, positions, g_mix, w_in, g_out_sb, g_out_moba, w_out, g_xattn, g_mem,
           w_xq, w_xkv, w_xo, g_ffn, w_gate_up, w_down, g_final):
    b, s, d = x.shape
    n_mem = mem.shape[1]
    n = b * s
    x0 = x.reshape(n, d)

    w_sb, w_mb = N_HEADS_SB * HEAD_DIM, N_HEADS_MOBA * HEAD_DIM
    q_scale = jnp.concatenate([jnp.full((w_sb,), ATTN_SCALE, F32), jnp.ones((2 * w_sb,), F32),
                               jnp.full((w_mb,), ATTN_SCALE, F32), jnp.ones((2 * w_mb,), F32)])
    qkv = _in_proj(x0, g_mix, w_in.astype(BF16), q_scale).reshape(b, s, -1)
    cos, sin = _rope_tables(positions)
    o_sb = _sb_attention(qkv, g_out_sb)
    o_mb = _moba_attention(qkv, cos, sin, g_out_moba)

    m = _rmsnorm(mem.reshape(b * n_mem, d), g_mem, BF16)
    kv = _matmul(m, w_xkv.astype(BF16), BF16, name="xkv_proj").reshape(b, n_mem, -1)

    x2, h = _mixer_out_xattn(o_sb.reshape(n, -1), o_mb.reshape(n, -1), w_out.astype(BF16), x0,
                             g_xattn, w_xq.astype(BF16), kv, w_xo.astype(BF16), g_ffn, s)

    a = _swiglu_up(h, w_gate_up)
    return _down_final(a, w_down.astype(BF16), x2, g_final).reshape(b, s, d)
```

```python
import functools

import jax
import jax.numpy as jnp
from jax import lax
from jax.experimental import pallas as pl
from jax.experimental.pallas import tpu as pltpu

F32 = jnp.float32
BF16 = jnp.bfloat16

HEAD_DIM = 128
N_HEADS_SB = 8
N_HEADS_MOBA = 8
N_HEADS_X = 4
MOBA_BLOCK = 256
MOBA_TOP_K = 3
ROPE_DIM = HEAD_DIM // 4
ROPE_THETA = 500000.0
EPS = 1e-6
ATTN_SCALE = HEAD_DIM ** -0.5
LOG2E = 1.4426950408889634

V7X_VMEM_BYTES = 64 * 1024 * 1024
VMEM_LIMIT_BYTES = V7X_VMEM_BYTES * 7 // 8

NORM_ROWS = 512
MM_TM = 1024
MM_TN = 1024
FFN_TN = 512
FUSED_TM = 512
DOWN_TM = 256
ATT_TQ = 256
MOBA_HEADS_PER_STEP = 1
NEG_BIG = -1e30

_NT = (((1,), (1,)), ((), ()))


def _params(n_axes):
    return pltpu.CompilerParams(
        dimension_semantics=("arbitrary",) * n_axes,
        vmem_limit_bytes=VMEM_LIMIT_BYTES)


def _dot(a, b):
    return jnp.dot(a, b, preferred_element_type=F32)


def _dot_nt(a, b):
    return lax.dot_general(a, b, _NT, preferred_element_type=F32)


def _rmsnorm_rows(x, g):
    ms = jnp.mean(x * x, axis=-1, keepdims=True)
    return x * lax.rsqrt(ms + EPS) * g


def _rmsnorm_kernel(x_ref, g_ref, o_ref):
    o_ref[...] = _rmsnorm_rows(x_ref[...].astype(F32), g_ref[...]).astype(o_ref.dtype)


def _rmsnorm(x, g, out_dtype):
    n, d = x.shape
    return pl.pallas_call(
        _rmsnorm_kernel,
        out_shape=jax.ShapeDtypeStruct((n, d), out_dtype),
        grid=(n // NORM_ROWS,),
        in_specs=[pl.BlockSpec((NORM_ROWS, d), lambda i: (i, 0)),
                  pl.BlockSpec((1, d), lambda i: (0, 0))],
        out_specs=pl.BlockSpec((NORM_ROWS, d), lambda i: (i, 0)),
        compiler_params=_params(1),
        name="rmsnorm",
    )(x, g.reshape(1, d))


def _matmul_kernel(a_ref, w_ref, o_ref):
    o_ref[...] = _dot(a_ref[...], w_ref[...]).astype(o_ref.dtype)


def _matmul(a, w, out_dtype, tm=MM_TM, tn=MM_TN, name="matmul"):
    m, k = a.shape
    n = w.shape[1]
    tm, tn = min(tm, m), min(tn, n)
    return pl.pallas_call(
        _matmul_kernel,
        out_shape=jax.ShapeDtypeStruct((m, n), out_dtype),
        grid=(m // tm, n // tn),
        in_specs=[pl.BlockSpec((tm, k), lambda i, j: (i, 0)),
                  pl.BlockSpec((k, tn), lambda i, j: (0, j))],
        out_specs=pl.BlockSpec((tm, tn), lambda i, j: (i, j)),
        compiler_params=_params(2),
        name=name,
    )(a, w)


def _in_proj_kernel(rot_lo, rot_hi, x_ref, g_ref, w_ref, s_ref, cos_ref, sin_ref, o_ref, h_ref):
    j = pl.program_id(1)

    @pl.when(j == 0)
    def _():
        h_ref[...] = _rmsnorm_rows(x_ref[...], g_ref[...]).astype(h_ref.dtype)

    rotary = (j >= rot_lo) & (j < rot_hi)

    @pl.when(jnp.logical_not(rotary))
    def _():
        o_ref[...] = (_dot(h_ref[...], w_ref[...]) * s_ref[...]).astype(o_ref.dtype)

    @pl.when(rotary)
    def _():
        y = _dot(h_ref[...], w_ref[...]) * s_ref[...]
        cos, sin = cos_ref[...], sin_ref[...]
        for hh in range(y.shape[1] // HEAD_DIM):
            cols = slice(hh * HEAD_DIM, (hh + 1) * HEAD_DIM)
            o_ref[:, cols] = _rotate(y[:, cols], cos, sin).astype(o_ref.dtype)


def _in_proj(x, g, w, col_scale, cos, sin, rot_cols):
    m, k = x.shape
    n = w.shape[1]
    tm, tn = MM_TM, MM_TN
    assert rot_cols[0] % tn == 0 and rot_cols[1] % tn == 0
    body = functools.partial(_in_proj_kernel, rot_cols[0] // tn, rot_cols[1] // tn)
    return pl.pallas_call(
        body,
        out_shape=jax.ShapeDtypeStruct((m, n), BF16),
        grid=(m // tm, n // tn),
        in_specs=[pl.BlockSpec((tm, k), lambda i, j: (i, 0)),
                  pl.BlockSpec((1, k), lambda i, j: (0, 0)),
                  pl.BlockSpec((k, tn), lambda i, j: (0, j)),
                  pl.BlockSpec((1, tn), lambda i, j: (0, j)),
                  pl.BlockSpec((tm, HEAD_DIM), lambda i, j: (i, 0)),
                  pl.BlockSpec((tm, HEAD_DIM), lambda i, j: (i, 0))],
        out_specs=pl.BlockSpec((tm, tn), lambda i, j: (i, j)),
        scratch_shapes=[pltpu.VMEM((tm, k), BF16)],
        compiler_params=_params(2),
        name="in_proj",
    )(x, g.reshape(1, k), w, col_scale.reshape(1, n), cos, sin)


def _mixer_xattn_kernel(osb_ref, omb_ref, wout_ref, x_ref, gx_ref, wq_ref, kv_ref, wo_ref, gf_ref,
                        x2_ref, h3_ref):
    k1 = osb_ref.shape[1]
    x1 = x_ref[...] + _dot(osb_ref[...], wout_ref[:k1, :]) + _dot(omb_ref[...], wout_ref[k1:, :])
    h2 = _rmsnorm_rows(x1, gx_ref[...]).astype(BF16)
    q = (_dot(h2, wq_ref[...]) * ATTN_SCALE).astype(BF16)
    w = wq_ref.shape[1]
    heads = []
    for hh in range(w // HEAD_DIM):
        cols = slice(hh * HEAD_DIM, (hh + 1) * HEAD_DIM)
        s = _dot_nt(q[:, cols], kv_ref[0, :, cols])
        p = jnp.exp(s - jnp.max(s, axis=-1, keepdims=True))
        l = jnp.sum(p, axis=-1, keepdims=True)
        v = kv_ref[0, :, w + hh * HEAD_DIM: w + (hh + 1) * HEAD_DIM]
        heads.append((_dot(p.astype(BF16), v) / l).astype(BF16))
    x2 = x1 + _dot(jnp.concatenate(heads, axis=1), wo_ref[...])
    x2_ref[...] = x2
    h3_ref[...] = _rmsnorm_rows(x2, gf_ref[...]).astype(h3_ref.dtype)


def _mixer_out_xattn(o_sb, o_mb, w_out, x, g_xattn, w_xq, kv, w_xo, g_ffn, rows_per_batch):
    m, d = x.shape
    k1 = o_sb.shape[1]
    wx = w_xq.shape[1]
    n_mem = kv.shape[1]
    tm = FUSED_TM
    assert rows_per_batch % tm == 0
    tiles_per_batch = rows_per_batch // tm
    const = lambda shape: pl.BlockSpec(shape, lambda i: (0,) * len(shape), pipeline_mode=pl.Buffered(1))
    rows = lambda width: pl.BlockSpec((tm, width), lambda i: (i, 0))
    return pl.pallas_call(
        _mixer_xattn_kernel,
        out_shape=(jax.ShapeDtypeStruct((m, d), F32), jax.ShapeDtypeStruct((m, d), BF16)),
        grid=(m // tm,),
        in_specs=[rows(k1), rows(k1), const((2 * k1, d)), rows(d), const((1, d)), const((d, wx)),
                  pl.BlockSpec((1, n_mem, 2 * wx), lambda i: (i // tiles_per_batch, 0, 0)),
                  const((wx, d)), const((1, d))],
        out_specs=(rows(d), rows(d)),
        compiler_params=_params(1),
        name="mixer_out_xattn",
    )(o_sb, o_mb, w_out, x, g_xattn.reshape(1, d), w_xq, kv, w_xo, g_ffn.reshape(1, d))


def _down_final_kernel(a_ref, w_ref, r_ref, g_ref, o_ref):
    x3 = r_ref[...] + _dot(a_ref[...], w_ref[...])
    o_ref[...] = _rmsnorm_rows(x3, g_ref[...])


def _down_final(a, w, residual, g):
    m, k = a.shape
    n = w.shape[1]
    tm = DOWN_TM
    return pl.pallas_call(
        _down_final_kernel,
        out_shape=jax.ShapeDtypeStruct((m, n), F32),
        grid=(m // tm,),
        in_specs=[pl.BlockSpec((tm, k), lambda i: (i, 0)),
                  pl.BlockSpec((k, n), lambda i: (0, 0), pipeline_mode=pl.Buffered(1)),
                  pl.BlockSpec((tm, n), lambda i: (i, 0)),
                  pl.BlockSpec((1, n), lambda i: (0, 0))],
        out_specs=pl.BlockSpec((tm, n), lambda i: (i, 0)),
        compiler_params=_params(1),
        name="down_final",
    )(a, w, residual, g.reshape(1, n))


def _swiglu_kernel(a_ref, wg_ref, wu_ref, o_ref, wg_bf, wu_bf):
    @pl.when(pl.program_id(1) == 0)
    def _():
        wg_bf[...] = wg_ref[...].astype(BF16)
        wu_bf[...] = wu_ref[...].astype(BF16)

    a = a_ref[...]
    g = _dot(a, wg_bf[...])
    u = _dot(a, wu_bf[...])
    o_ref[...] = (g / (1.0 + jnp.exp(-g)) * u).astype(o_ref.dtype)


def _swiglu_up(h, w_gate_up):
    m, k = h.shape
    d_ff = w_gate_up.shape[1] // 2
    tm, tn = MM_TM, FFN_TN
    nj = d_ff // tn
    return pl.pallas_call(
        _swiglu_kernel,
        out_shape=jax.ShapeDtypeStruct((m, d_ff), BF16),
        grid=(nj, m // tm),
        in_specs=[pl.BlockSpec((tm, k), lambda j, i: (i, 0)),
                  pl.BlockSpec((k, tn), lambda j, i: (0, j)),
                  pl.BlockSpec((k, tn), lambda j, i: (0, j + nj))],
        out_specs=pl.BlockSpec((tm, tn), lambda j, i: (i, j)),
        scratch_shapes=[pltpu.VMEM((k, tn), BF16), pltpu.VMEM((k, tn), BF16)],
        compiler_params=_params(2),
        name="swiglu_up",
    )(h, w_gate_up, w_gate_up)


def _rope_table_kernel(pos_ref, invf_ref, sign_ref, cos_ref, sin_ref):
    ang = pos_ref[0].astype(F32) * invf_ref[...]
    cos_ref[0] = jnp.cos(ang)
    sin_ref[0] = jnp.sin(ang) * sign_ref[...]


def _rope_tables(positions):
    b, s = positions.shape
    half = ROPE_DIM // 2
    inv_freq = ROPE_THETA ** (-jnp.arange(0, ROPE_DIM, 2, dtype=F32) / ROPE_DIM)
    pad = jnp.zeros((HEAD_DIM - ROPE_DIM,), F32)
    invf = jnp.concatenate([inv_freq, inv_freq, pad]).reshape(1, HEAD_DIM)
    sign = jnp.concatenate([-jnp.ones((half,), F32), jnp.ones((half,), F32), pad]).reshape(1, HEAD_DIM)
    tab = jax.ShapeDtypeStruct((b, s, HEAD_DIM), F32)
    return pl.pallas_call(
        _rope_table_kernel,
        out_shape=(tab, tab),
        grid=(b,),
        in_specs=[pl.BlockSpec((1, s, 1), lambda i: (i, 0, 0)),
                  pl.BlockSpec((1, HEAD_DIM), lambda i: (0, 0)),
                  pl.BlockSpec((1, HEAD_DIM), lambda i: (0, 0))],
        out_specs=(pl.BlockSpec((1, s, HEAD_DIM), lambda i: (i, 0, 0)),
                   pl.BlockSpec((1, s, HEAD_DIM), lambda i: (i, 0, 0))),
        compiler_params=_params(1),
        name="rope_tables",
    )(positions.reshape(b, s, 1), invf, sign)


def _rotate(x, cos, sin_signed):
    half = ROPE_DIM // 2
    lane = lax.broadcasted_iota(jnp.int32, x.shape, 1)
    partner = jnp.where(lane < half,
                        pltpu.roll(x, HEAD_DIM - half, axis=1),
                        pltpu.roll(x, half, axis=1))
    return x * cos + partner * sin_signed


def _sb_kernel(q_ref, k_ref, v_ref, g_ref, o_ref, kt_ref):
    t = ATT_TQ
    n_tiles = k_ref.shape[1] // t
    row = lax.broadcasted_iota(jnp.int32, (t, t), 0)
    col = lax.broadcasted_iota(jnp.int32, (t, t), 1)
    past = col < row
    neg_from = jnp.where(row >= col, -1.0, 0.0).astype(BF16)
    g = g_ref[...]

    for j in range(n_tiles):
        kt_ref[:, j * t:(j + 1) * t] = k_ref[0, j * t:(j + 1) * t, :].astype(F32).T.astype(BF16)

    def logits(i):
        return _dot(q_ref[0, i * t:(i + 1) * t, :], kt_ref[:, :(i + 1) * t])

    z_next = logits(0)
    for i in range(n_tiles):
        keys = (i + 1) * t
        z = z_next
        if i + 1 < n_tiles:
            z_next = logits(i + 1)
        sp = jnp.maximum(z, 0.0) + jnp.log2(1.0 + jnp.exp2(-jnp.abs(z)))
        carry = jnp.zeros((t, 1), F32)
        w = [None] * (i + 1)
        for j in reversed(range(i + 1)):
            cols = slice(j * t, (j + 1) * t)
            sp_j = sp[:, cols]
            if j == i:
                sp_j = jnp.where(past, sp_j, 0.0)
            tail = _dot(sp_j.astype(BF16), neg_from)
            w_j = jnp.exp2(z[:, cols] + tail + carry)
            if j == i:
                w_j = jnp.where(past, w_j, 0.0)
            w[j] = w_j.astype(BF16)
            carry = carry + tail[:, 0:1]
        acc = _dot(jnp.concatenate(w, axis=1), v_ref[0, :keys, :])
        o_ref[0, i * t:(i + 1) * t, :] = _rmsnorm_rows(acc, g).astype(o_ref.dtype)


def _sb_attention(qkv, g_out):
    b, s, _ = qkv.shape
    h = N_HEADS_SB
    blk = lambda base: pl.BlockSpec((1, s, HEAD_DIM), lambda bi, hi: (bi, 0, base + hi))
    return pl.pallas_call(
        _sb_kernel,
        out_shape=jax.ShapeDtypeStruct((b, s, h * HEAD_DIM), BF16),
        grid=(b, h),
        in_specs=[blk(0), blk(h), blk(2 * h),
                  pl.BlockSpec((1, HEAD_DIM), lambda bi, hi: (0, hi))],
        out_specs=blk(0),
        scratch_shapes=[pltpu.VMEM((HEAD_DIM, s), BF16)],
        compiler_params=_params(2),
        name="sb_attention",
    )(qkv, qkv, qkv, g_out.reshape(1, h * HEAD_DIM))


def _moba_kernel(q_ref, k_ref, v_ref, g_ref, o_ref, ka_ref, km_ref):
    blk = MOBA_BLOCK
    nb = k_ref.shape[1] // blk
    d = HEAD_DIM
    heads = k_ref.shape[2] // d
    row = lax.broadcasted_iota(jnp.int32, (blk, blk), 0)
    col = lax.broadcasted_iota(jnp.int32, (blk, blk), 1)
    causal = col <= row
    sub = lax.broadcasted_iota(jnp.int32, (d, blk), 0)
    blk_id = lax.broadcasted_iota(jnp.int32, (km_ref.shape[1], blk), 0)

    km_ref[...] = jnp.zeros_like(km_ref)
    for j in range(nb):
        rows = slice(j * blk, (j + 1) * blk)
        for hh in range(heads):
            cols = slice(hh * d, (hh + 1) * d)
            kr = k_ref[0, rows, cols].astype(F32)
            ka_ref[hh, :d, rows] = kr.T.astype(BF16)
            ka_ref[hh, d:, rows] = jnp.where(sub == j, 1.0, 0.0).astype(BF16)
            km_ref[hh, j:j + 1, :] = jnp.mean(kr, axis=0, keepdims=True)

    def scores(hh, n):
        cols = slice(hh * d, (hh + 1) * d)
        rows = slice(n * blk, (n + 1) * blk)
        keys = (n + 1) * blk
        q = q_ref[0, rows, cols]
        if n <= MOBA_TOP_K:
            lhs, k_rows = q, slice(0, d)
        else:
            km = km_ref[hh]
            km_hi = km.astype(BF16)
            km_lo = (km - km_hi.astype(F32)).astype(BF16)
            gate = _dot_nt(km_hi, q) + _dot_nt(km_lo, q)
            rank = jnp.zeros(gate.shape, F32)
            for c in range(n):
                gc = gate[c:c + 1, :]
                beats = (gc > gate) | ((gc == gate) & (c < blk_id))
                rank = rank + jnp.where(beats, 1.0, 0.0)
            allowed = ((rank < MOBA_TOP_K) & (blk_id < n)) | (blk_id == n)
            bias_t = jnp.where(allowed, 0.0, NEG_BIG)
            bias_t = jnp.concatenate([bias_t, jnp.zeros((d - bias_t.shape[0], blk), F32)], axis=0)
            lhs = jnp.concatenate([q, bias_t.T.astype(BF16)], axis=1)
            k_rows = slice(0, 2 * d)
        s = []
        m_elem = None
        for j in range(n + 1):
            s_j = _dot(lhs, ka_ref[hh, k_rows, j * blk:(j + 1) * blk])
            if j == n:
                s_j = jnp.where(causal, s_j, -jnp.inf)
            s.append(s_j)
            m_elem = s_j if m_elem is None else jnp.maximum(m_elem, s_j)
        return s, jnp.max(m_elem, axis=-1, keepdims=True)

    def attend(hh, n, s, m):
        cols = slice(hh * d, (hh + 1) * d)
        rows = slice(n * blk, (n + 1) * blk)
        l_elem = None
        acc = None
        for j in range(n + 1):
            p_j = jnp.exp2(s[j] - m)
            pv = _dot(p_j.astype(BF16), v_ref[0, j * blk:(j + 1) * blk, cols])
            l_elem = p_j if l_elem is None else l_elem + p_j
            acc = pv if acc is None else acc + pv
        l = jnp.sum(l_elem, axis=-1, keepdims=True)
        o_ref[0, rows, cols] = _rmsnorm_rows(acc / l, g_ref[:, cols]).astype(o_ref.dtype)

    items = [(hh, n) for n in range(nb) for hh in range(heads)]
    pending = None
    for item in items:
        s, m = scores(*item)
        if pending is not None:
            attend(*pending)
        pending = (*item, s, m)
    attend(*pending)


def _moba_attention(qkv, g_out):
    b, s, _ = qkv.shape
    h = N_HEADS_MOBA
    hps = MOBA_HEADS_PER_STEP
    groups = h // hps
    width = hps * HEAD_DIM
    base = 3 * N_HEADS_SB // hps
    km_rows = 16
    assert s % MOBA_BLOCK == 0 and s // MOBA_BLOCK <= km_rows
    blk = lambda off: pl.BlockSpec((1, s, width), lambda bi, gi: (bi, 0, off + gi))
    return pl.pallas_call(
        _moba_kernel,
        out_shape=jax.ShapeDtypeStruct((b, s, h * HEAD_DIM), BF16),
        grid=(b, groups),
        in_specs=[blk(base), blk(base + groups), blk(base + 2 * groups),
                  pl.BlockSpec((1, width), lambda bi, gi: (0, gi))],
        out_specs=blk(0),
        scratch_shapes=[pltpu.VMEM((hps, 2 * HEAD_DIM, s), BF16),
                        pltpu.VMEM((hps, km_rows, HEAD_DIM), F32)],
        compiler_params=_params(2),
        name="moba_attention",
    )(qkv, qkv, qkv, g_out.reshape(1, h * HEAD_DIM))


def kernel(x, mem, positions, g_mix, w_in, g_out_sb, g_out_moba, w_out, g_xattn, g_mem,
           w_xq, w_xkv, w_xo, g_ffn, w_gate_up, w_down, g_final):
    b, s, d = x.shape
    n_mem = mem.shape[1]
    n = b * s
    x0 = x.reshape(n, d)

    w_sb, w_mb = N_HEADS_SB * HEAD_DIM, N_HEADS_MOBA * HEAD_DIM
    q_scale = jnp.concatenate([jnp.full((w_sb,), ATTN_SCALE * LOG2E, F32), jnp.ones((2 * w_sb,), F32),
                               jnp.full((w_mb,), ATTN_SCALE * LOG2E, F32), jnp.ones((2 * w_mb,), F32)])
    cos, sin = _rope_tables(positions)
    rot_cols = (3 * w_sb, 3 * w_sb + 2 * w_mb)
    qkv = _in_proj(x0, g_mix, w_in.astype(BF16), q_scale, cos.reshape(n, -1), sin.reshape(n, -1),
                   rot_cols).reshape(b, s, -1)
    o_sb = _sb_attention(qkv, g_out_sb)
    o_mb = _moba_attention(qkv, g_out_moba)

    m = _rmsnorm(mem.reshape(b * n_mem, d), g_mem, BF16)
    kv = _matmul(m, w_xkv.astype(BF16), BF16, name="xkv_proj").reshape(b, n_mem, -1)

    x2, h = _mixer_out_xattn(o_sb.reshape(n, -1), o_mb.reshape(n, -1), w_out.astype(BF16), x0,
                             g_xattn, w_xq.astype(BF16), kv, w_xo.astype(BF16), g_ffn, s)

    a = _swiglu_up(h, w_gate_up)
    return _down_final(a, w_down.astype(BF16), x2, g_final).reshape(b, s, d)
```

```python
import functools

import jax
import jax.numpy as jnp
from jax import lax
from jax.experimental import pallas as pl
from jax.experimental.pallas import tpu as pltpu

F32 = jnp.float32
BF16 = jnp.bfloat16

HEAD_DIM = 128
N_HEADS_SB = 8
N_HEADS_MOBA = 8
N_HEADS_X = 4
MOBA_BLOCK = 256
MOBA_TOP_K = 3
ROPE_DIM = HEAD_DIM // 4
ROPE_THETA = 500000.0
EPS = 1e-6
ATTN_SCALE = HEAD_DIM ** -0.5
LOG2E = 1.4426950408889634

V7X_VMEM_BYTES = 64 * 1024 * 1024
VMEM_LIMIT_BYTES = V7X_VMEM_BYTES * 7 // 8

MM_TM = 1024
MM_TN = 1024
FFN_TM = 1024
FFN_TN = 512
FUSED_TM = 512
DOWN_TM = 256
ATT_TQ = 256
MOBA_HEADS_PER_STEP = 1
NEG_BIG = -1e30

_NT = (((1,), (1,)), ((), ()))


def _params(n_axes):
    return pltpu.CompilerParams(
        dimension_semantics=("arbitrary",) * n_axes,
        vmem_limit_bytes=VMEM_LIMIT_BYTES)


def _dot(a, b):
    return jnp.dot(a, b, preferred_element_type=F32)


def _dot_nt(a, b):
    return lax.dot_general(a, b, _NT, preferred_element_type=F32)


def _rmsnorm_rows(x, g):
    ms = jnp.mean(x * x, axis=-1, keepdims=True)
    return x * lax.rsqrt(ms + EPS) * g


def _norm_proj_kernel(x_ref, g_ref, w_ref, o_ref):
    h = _rmsnorm_rows(x_ref[...], g_ref[...]).astype(BF16)
    o_ref[...] = _dot(h, w_ref[...].astype(BF16)).astype(o_ref.dtype)


def _norm_proj(x, g, w):
    m, k = x.shape
    n = w.shape[1]
    tm = min(MM_TM, m)
    return pl.pallas_call(
        _norm_proj_kernel,
        out_shape=jax.ShapeDtypeStruct((m, n), BF16),
        grid=(m // tm,),
        in_specs=[pl.BlockSpec((tm, k), lambda i: (i, 0)),
                  pl.BlockSpec((1, k), lambda i: (0, 0)),
                  pl.BlockSpec((k, n), lambda i: (0, 0), pipeline_mode=pl.Buffered(1))],
        out_specs=pl.BlockSpec((tm, n), lambda i: (i, 0)),
        compiler_params=_params(1),
        name="mem_kv_proj",
    )(x, g.reshape(1, k), w)


def _in_proj_kernel(rot_lo, rot_hi, x_ref, g_ref, w_ref, s_ref, pos_ref, invf_ref, sign_ref, o_ref,
                    h_ref, cos_ref, sin_ref):
    j = pl.program_id(1)

    assert rot_lo > 0

    @pl.when(j == 0)
    def _():
        h = _rmsnorm_rows(x_ref[...], g_ref[...]).astype(h_ref.dtype)
        h_ref[...] = h
        o_ref[...] = (_dot(h, w_ref[...].astype(BF16)) * s_ref[...]).astype(o_ref.dtype)
        ang = pos_ref[...].astype(F32) * invf_ref[...]
        cos_ref[...] = jnp.cos(ang)
        sin_ref[...] = jnp.sin(ang) * sign_ref[...]

    rotary = (j >= rot_lo) & (j < rot_hi)

    @pl.when((j > 0) & jnp.logical_not(rotary))
    def _():
        o_ref[...] = (_dot(h_ref[...], w_ref[...].astype(BF16)) * s_ref[...]).astype(o_ref.dtype)

    @pl.when(rotary)
    def _():
        half = h_ref.shape[0] // 2
        w = w_ref[...].astype(BF16)
        for r in range(2):
            rows = slice(r * half, (r + 1) * half)
            y = _dot(h_ref[rows, :], w) * s_ref[...]
            cos, sin = cos_ref[rows, :], sin_ref[rows, :]
            for hh in range(y.shape[1] // HEAD_DIM):
                cols = slice(hh * HEAD_DIM, (hh + 1) * HEAD_DIM)
                o_ref[rows, cols] = _rotate(y[:, cols], cos, sin).astype(o_ref.dtype)


def _in_proj(x, g, w, col_scale, positions, rot_cols):
    m, k = x.shape
    n = w.shape[1]
    tm, tn = MM_TM, MM_TN
    assert rot_cols[0] % tn == 0 and rot_cols[1] % tn == 0
    body = functools.partial(_in_proj_kernel, rot_cols[0] // tn, rot_cols[1] // tn)
    half = ROPE_DIM // 2
    inv_freq = ROPE_THETA ** (-jnp.arange(0, ROPE_DIM, 2, dtype=F32) / ROPE_DIM)
    pad = jnp.zeros((HEAD_DIM - ROPE_DIM,), F32)
    invf = jnp.concatenate([inv_freq, inv_freq, pad]).reshape(1, HEAD_DIM)
    sign = jnp.concatenate([-jnp.ones((half,), F32), jnp.ones((half,), F32), pad]).reshape(1, HEAD_DIM)
    row_vec = pl.BlockSpec((1, HEAD_DIM), lambda i, j: (0, 0))
    return pl.pallas_call(
        body,
        out_shape=jax.ShapeDtypeStruct((m, n), BF16),
        grid=(m // tm, n // tn),
        in_specs=[pl.BlockSpec((tm, k), lambda i, j: (i, 0)),
                  pl.BlockSpec((1, k), lambda i, j: (0, 0)),
                  pl.BlockSpec((k, tn), lambda i, j: (0, j)),
                  pl.BlockSpec((1, tn), lambda i, j: (0, j)),
                  pl.BlockSpec((tm, 1), lambda i, j: (i, 0)),
                  row_vec, row_vec],
        out_specs=pl.BlockSpec((tm, tn), lambda i, j: (i, j)),
        scratch_shapes=[pltpu.VMEM((tm, k), BF16),
                        pltpu.VMEM((tm, HEAD_DIM), F32), pltpu.VMEM((tm, HEAD_DIM), F32)],
        compiler_params=_params(2),
        name="in_proj",
    )(x, g.reshape(1, k), w, col_scale.reshape(1, n), positions.reshape(m, 1), invf, sign)


def _mixer_xattn_kernel(osb_ref, omb_ref, wout_ref, x_ref, gx_ref, wq_ref, kv_ref, wo_ref, gf_ref,
                        x2_ref, h3_ref):
    k1 = osb_ref.shape[1]
    w = wq_ref.shape[1]
    half = x_ref.shape[0] // 2
    halves = [slice(r * half, (r + 1) * half) for r in range(2)]

    def out_proj(rows):
        return (x_ref[rows, :] + _dot(osb_ref[rows, :], wout_ref[:k1, :])
                + _dot(omb_ref[rows, :], wout_ref[k1:, :]))

    def q_proj(x1):
        h2 = _rmsnorm_rows(x1, gx_ref[...]).astype(BF16)
        return (_dot(h2, wq_ref[...]) * (ATTN_SCALE * LOG2E)).astype(BF16)

    def attend(q):
        heads = []
        for hh in range(w // HEAD_DIM):
            cols = slice(hh * HEAD_DIM, (hh + 1) * HEAD_DIM)
            s = _dot_nt(q[:, cols], kv_ref[0, :, cols])
            p = jnp.exp2(s - jnp.max(s, axis=-1, keepdims=True))
            l = jnp.sum(p, axis=-1, keepdims=True)
            v = kv_ref[0, :, w + hh * HEAD_DIM: w + (hh + 1) * HEAD_DIM]
            heads.append((_dot(p.astype(BF16), v) / l).astype(BF16))
        return jnp.concatenate(heads, axis=1)

    x1 = [out_proj(rows) for rows in halves]
    q = [q_proj(v) for v in x1]
    ox = [attend(v) for v in q]
    for rows, x1_r, ox_r in zip(halves, x1, ox):
        x2 = x1_r + _dot(ox_r, wo_ref[...])
        x2_ref[rows, :] = x2
        h3_ref[rows, :] = _rmsnorm_rows(x2, gf_ref[...]).astype(h3_ref.dtype)


def _mixer_out_xattn(o_sb, o_mb, w_out, x, g_xattn, w_xq, kv, w_xo, g_ffn, rows_per_batch):
    m, d = x.shape
    k1 = o_sb.shape[1]
    wx = w_xq.shape[1]
    n_mem = kv.shape[1]
    tm = FUSED_TM
    assert rows_per_batch % tm == 0
    tiles_per_batch = rows_per_batch // tm
    const = lambda shape: pl.BlockSpec(shape, lambda i: (0,) * len(shape), pipeline_mode=pl.Buffered(1))
    rows = lambda width: pl.BlockSpec((tm, width), lambda i: (i, 0))
    return pl.pallas_call(
        _mixer_xattn_kernel,
        out_shape=(jax.ShapeDtypeStruct((m, d), F32), jax.ShapeDtypeStruct((m, d), BF16)),
        grid=(m // tm,),
        in_specs=[rows(k1), rows(k1), const((2 * k1, d)), rows(d), const((1, d)), const((d, wx)),
                  pl.BlockSpec((1, n_mem, 2 * wx), lambda i: (i // tiles_per_batch, 0, 0)),
                  const((wx, d)), const((1, d))],
        out_specs=(rows(d), rows(d)),
        compiler_params=_params(1),
        name="mixer_out_xattn",
    )(o_sb, o_mb, w_out, x, g_xattn.reshape(1, d), w_xq, kv, w_xo, g_ffn.reshape(1, d))


def _down_final_kernel(a_ref, w_ref, r_ref, g_ref, o_ref):
    x3 = r_ref[...] + _dot(a_ref[...], w_ref[...])
    o_ref[...] = _rmsnorm_rows(x3, g_ref[...])


def _down_final(a, w, residual, g):
    m, k = a.shape
    n = w.shape[1]
    tm = DOWN_TM
    return pl.pallas_call(
        _down_final_kernel,
        out_shape=jax.ShapeDtypeStruct((m, n), F32),
        grid=(m // tm,),
        in_specs=[pl.BlockSpec((tm, k), lambda i: (i, 0)),
                  pl.BlockSpec((k, n), lambda i: (0, 0), pipeline_mode=pl.Buffered(1)),
                  pl.BlockSpec((tm, n), lambda i: (i, 0)),
                  pl.BlockSpec((1, n), lambda i: (0, 0))],
        out_specs=pl.BlockSpec((tm, n), lambda i: (i, 0)),
        compiler_params=_params(1),
        name="down_final",
    )(a, w, residual, g.reshape(1, n))


def _swiglu_kernel(a_ref, wg_ref, wu_ref, o_ref, w_bf):
    tn = wg_ref.shape[1]

    @pl.when(pl.program_id(1) == 0)
    def _():
        w_bf[:, :tn] = wg_ref[...].astype(BF16)
        w_bf[:, tn:] = wu_ref[...].astype(BF16)

    gu = _dot(a_ref[...], w_bf[...])
    g, u = gu[:, :tn], gu[:, tn:]
    o_ref[...] = (g / (1.0 + jnp.exp(-g)) * u).astype(o_ref.dtype)


def _swiglu_up(h, w_gate_up):
    m, k = h.shape
    d_ff = w_gate_up.shape[1] // 2
    tm, tn = FFN_TM, FFN_TN
    nj = d_ff // tn
    return pl.pallas_call(
        _swiglu_kernel,
        out_shape=jax.ShapeDtypeStruct((m, d_ff), BF16),
        grid=(nj, m // tm),
        in_specs=[pl.BlockSpec((tm, k), lambda j, i: (i, 0)),
                  pl.BlockSpec((k, tn), lambda j, i: (0, j)),
                  pl.BlockSpec((k, tn), lambda j, i: (0, j + nj))],
        out_specs=pl.BlockSpec((tm, tn), lambda j, i: (i, j)),
        scratch_shapes=[pltpu.VMEM((k, 2 * tn), BF16)],
        compiler_params=_params(2),
        name="swiglu_up",
    )(h, w_gate_up, w_gate_up)


def _rotate(x, cos, sin_signed):
    half = ROPE_DIM // 2
    lane = lax.broadcasted_iota(jnp.int32, x.shape, 1)
    partner = jnp.where(lane < half,
                        pltpu.roll(x, HEAD_DIM - half, axis=1),
                        pltpu.roll(x, half, axis=1))
    return x * cos + partner * sin_signed


def _sb_kernel(q_ref, k_ref, v_ref, g_ref, o_ref, kt_ref):
    t = ATT_TQ
    n_tiles = k_ref.shape[1] // t
    row = lax.broadcasted_iota(jnp.int32, (t, t), 0)
    col = lax.broadcasted_iota(jnp.int32, (t, t), 1)
    past = col < row
    neg_from = jnp.where(row >= col, -1.0, 0.0).astype(BF16)
    g = g_ref[...]

    for j in range(n_tiles):
        kt_ref[:, j * t:(j + 1) * t] = k_ref[0, j * t:(j + 1) * t, :].astype(F32).T.astype(BF16)

    def logits(i):
        return _dot(q_ref[0, i * t:(i + 1) * t, :], kt_ref[:, :(i + 1) * t])

    z_next = logits(0)
    for i in range(n_tiles):
        keys = (i + 1) * t
        z = z_next
        if i + 1 < n_tiles:
            z_next = logits(i + 1)
        sp = jnp.maximum(z, 0.0) + jnp.log2(1.0 + jnp.exp2(-jnp.abs(z)))
        carry = jnp.zeros((t, 1), F32)
        w = [None] * (i + 1)
        for j in reversed(range(i + 1)):
            cols = slice(j * t, (j + 1) * t)
            sp_j = sp[:, cols]
            if j == i:
                sp_j = jnp.where(past, sp_j, 0.0)
            tail = _dot(sp_j.astype(BF16), neg_from)
            w_j = jnp.exp2(z[:, cols] + tail + carry)
            if j == i:
                w_j = jnp.where(past, w_j, 0.0)
            w[j] = w_j.astype(BF16)
            carry = carry + tail[:, 0:1]
        acc = _dot(jnp.concatenate(w, axis=1), v_ref[0, :keys, :])
        o_ref[0, i * t:(i + 1) * t, :] = _rmsnorm_rows(acc, g).astype(o_ref.dtype)


def _sb_attention(qkv, g_out):
    b, s, _ = qkv.shape
    h = N_HEADS_SB
    blk = lambda base: pl.BlockSpec((1, s, HEAD_DIM), lambda bi, hi: (bi, 0, base + hi))
    return pl.pallas_call(
        _sb_kernel,
        out_shape=jax.ShapeDtypeStruct((b, s, h * HEAD_DIM), BF16),
        grid=(b, h),
        in_specs=[blk(0), blk(h), blk(2 * h),
                  pl.BlockSpec((1, HEAD_DIM), lambda bi, hi: (0, hi))],
        out_specs=blk(0),
        scratch_shapes=[pltpu.VMEM((HEAD_DIM, s), BF16)],
        compiler_params=_params(2),
        name="sb_attention",
    )(qkv, qkv, qkv, g_out.reshape(1, h * HEAD_DIM))


def _moba_kernel(q_ref, k_ref, v_ref, g_ref, o_ref, ka_ref, km_ref):
    blk = MOBA_BLOCK
    nb = k_ref.shape[1] // blk
    d = HEAD_DIM
    heads = k_ref.shape[2] // d
    row = lax.broadcasted_iota(jnp.int32, (blk, blk), 0)
    col = lax.broadcasted_iota(jnp.int32, (blk, blk), 1)
    causal = col <= row
    sub = lax.broadcasted_iota(jnp.int32, (d, blk), 0)
    blk_id = lax.broadcasted_iota(jnp.int32, (km_ref.shape[1], blk), 0)

    km_ref[...] = jnp.zeros_like(km_ref)
    for j in range(nb):
        rows = slice(j * blk, (j + 1) * blk)
        for hh in range(heads):
            cols = slice(hh * d, (hh + 1) * d)
            kr = k_ref[0, rows, cols].astype(F32)
            ka_ref[hh, :d, rows] = kr.T.astype(BF16)
            ka_ref[hh, d:, rows] = jnp.where(sub == j, 1.0, 0.0).astype(BF16)
            km_ref[hh, j:j + 1, :] = jnp.mean(kr, axis=0, keepdims=True)

    def scores(hh, n):
        cols = slice(hh * d, (hh + 1) * d)
        rows = slice(n * blk, (n + 1) * blk)
        keys = (n + 1) * blk
        q = q_ref[0, rows, cols]
        if n <= MOBA_TOP_K:
            lhs, k_rows = q, slice(0, d)
        else:
            km = km_ref[hh]
            km_hi = km.astype(BF16)
            km_lo = (km - km_hi.astype(F32)).astype(BF16)
            gate = _dot_nt(km_hi, q) + _dot_nt(km_lo, q)
            rank = jnp.zeros(gate.shape, F32)
            for c in range(n):
                gc = gate[c:c + 1, :]
                beats = (gc > gate) | ((gc == gate) & (c < blk_id))
                rank = rank + jnp.where(beats, 1.0, 0.0)
            allowed = ((rank < MOBA_TOP_K) & (blk_id < n)) | (blk_id == n)
            bias_t = jnp.where(allowed, 0.0, NEG_BIG)
            bias_t = jnp.concatenate([bias_t, jnp.zeros((d - bias_t.shape[0], blk), F32)], axis=0)
            lhs = jnp.concatenate([q, bias_t.T.astype(BF16)], axis=1)
            k_rows = slice(0, 2 * d)
        s = []
        m_elem = None
        for j in range(n + 1):
            s_j = _dot(lhs, ka_ref[hh, k_rows, j * blk:(j + 1) * blk])
            if j == n:
                s_j = jnp.where(causal, s_j, -jnp.inf)
            s.append(s_j)
            m_elem = s_j if m_elem is None else jnp.maximum(m_elem, s_j)
        return s, jnp.max(m_elem, axis=-1, keepdims=True)

    def attend(hh, n, s, m):
        cols = slice(hh * d, (hh + 1) * d)
        rows = slice(n * blk, (n + 1) * blk)
        l_elem = None
        acc = None
        for j in range(n + 1):
            p_j = jnp.exp2(s[j] - m)
            pv = _dot(p_j.astype(BF16), v_ref[0, j * blk:(j + 1) * blk, cols])
            l_elem = p_j if l_elem is None else l_elem + p_j
            acc = pv if acc is None else acc + pv
        l = jnp.sum(l_elem, axis=-1, keepdims=True)
        o_ref[0, rows, cols] = _rmsnorm_rows(acc / l, g_ref[:, cols]).astype(o_ref.dtype)

    items = [(hh, n) for n in range(nb) for hh in range(heads)]
    pending = None
    for item in items:
        s, m = scores(*item)
        if pending is not None:
            attend(*pending)
        pending = (*item, s, m)
    attend(*pending)


def _moba_attention(qkv, g_out):
    b, s, _ = qkv.shape
    h = N_HEADS_MOBA
    hps = MOBA_HEADS_PER_STEP
    groups = h // hps
    width = hps * HEAD_DIM
    base = 3 * N_HEADS_SB // hps
    km_rows = 16
    assert s % MOBA_BLOCK == 0 and s // MOBA_BLOCK <= km_rows
    blk = lambda off: pl.BlockSpec((1, s, width), lambda bi, gi: (bi, 0, off + gi))
    return pl.pallas_call(
        _moba_kernel,
        out_shape=jax.ShapeDtypeStruct((b, s, h * HEAD_DIM), BF16),
        grid=(b, groups),
        in_specs=[blk(base), blk(base + groups), blk(base + 2 * groups),
                  pl.BlockSpec((1, width), lambda bi, gi: (0, gi))],
        out_specs=blk(0),
        scratch_shapes=[pltpu.VMEM((hps, 2 * HEAD_DIM, s), BF16),
                        pltpu.VMEM((hps, km_rows, HEAD_DIM), F32)],
        compiler_params=_params(2),
        name="moba_attention",
    )(qkv, qkv, qkv, g_out.reshape(1, h * HEAD_DIM))


def kernel(x, mem, positions, g_mix, w_in, g_out_sb, g_out_moba, w_out, g_xattn, g_mem,
           w_xq, w_xkv, w_xo, g_ffn, w_gate_up, w_down, g_final):
    b, s, d = x.shape
    n_mem = mem.shape[1]
    n = b * s
    x0 = x.reshape(n, d)

    w_sb, w_mb = N_HEADS_SB * HEAD_DIM, N_HEADS_MOBA * HEAD_DIM
    q_scale = jnp.concatenate([jnp.full((w_sb,), ATTN_SCALE * LOG2E, F32), jnp.ones((2 * w_sb,), F32),
                               jnp.full((w_mb,), ATTN_SCALE * LOG2E, F32), jnp.ones((2 * w_mb,), F32)])
    rot_cols = (3 * w_sb, 3 * w_sb + 2 * w_mb)
    qkv = _in_proj(x0, g_mix, w_in, q_scale, positions, rot_cols).reshape(b, s, -1)
    o_sb = _sb_attention(qkv, g_out_sb)
    o_mb = _moba_attention(qkv, g_out_moba)

    kv = _norm_proj(mem.reshape(b * n_mem, d), g_mem, w_xkv).reshape(b, n_mem, -1)

    x2, h = _mixer_out_xattn(o_sb.reshape(n, -1), o_mb.reshape(n, -1), w_out.astype(BF16), x0,
                             g_xattn, w_xq.astype(BF16), kv, w_xo.astype(BF16), g_ffn, s)

    a = _swiglu_up(h, w_gate_up)
    return _down_final(a, w_down.astype(BF16), x2, g_final).reshape(b, s, d)
```

```python
import functools

import jax
import jax.numpy as jnp
from jax import lax
from jax.experimental import pallas as pl
from jax.experimental.pallas import tpu as pltpu

F32 = jnp.float32
BF16 = jnp.bfloat16

HEAD_DIM = 128
N_HEADS_SB = 8
N_HEADS_MOBA = 8
N_HEADS_X = 4
MOBA_BLOCK = 256
MOBA_TOP_K = 3
ROPE_DIM = HEAD_DIM // 4
ROPE_THETA = 500000.0
EPS = 1e-6
ATTN_SCALE = HEAD_DIM ** -0.5
LOG2E = 1.4426950408889634

V7X_VMEM_BYTES = 64 * 1024 * 1024
VMEM_LIMIT_BYTES = V7X_VMEM_BYTES * 7 // 8

MM_TM = 1024
MM_TN = 1024
IN_PROJ_ROW_CHUNKS = 4
IN_PROJ_ROT_CHUNKS = 4
FFN_TM = 1024
FFN_TN = 512
FFN_ROW_CHUNKS = 4
FUSED_TM = 512
DOWN_TM = 512
DOWN_ROW_CHUNKS = 2
ATT_TQ = 256
SB_HEADS_PER_STEP = 2
MOBA_HEADS_PER_STEP = 2
NEG_BIG = -1e30

_NT = (((1,), (1,)), ((), ()))


def _params(n_axes):
    return pltpu.CompilerParams(
        dimension_semantics=("arbitrary",) * n_axes,
        vmem_limit_bytes=VMEM_LIMIT_BYTES)


def _dot(a, b):
    return jnp.dot(a, b, preferred_element_type=F32)


def _dot_nt(a, b):
    return lax.dot_general(a, b, _NT, preferred_element_type=F32)


def _rmsnorm_rows(x, g):
    ms = jnp.mean(x * x, axis=-1, keepdims=True)
    return x * lax.rsqrt(ms + EPS) * g


def _norm_proj_kernel(x_ref, g_ref, w_ref, o_ref):
    h = _rmsnorm_rows(x_ref[...], g_ref[...]).astype(BF16)
    o_ref[...] = _dot(h, w_ref[...].astype(BF16)).astype(o_ref.dtype)


def _norm_proj(x, g, w):
    m, k = x.shape
    n = w.shape[1]
    tm = min(MM_TM, m)
    return pl.pallas_call(
        _norm_proj_kernel,
        out_shape=jax.ShapeDtypeStruct((m, n), BF16),
        grid=(m // tm,),
        in_specs=[pl.BlockSpec((tm, k), lambda i: (i, 0)),
                  pl.BlockSpec((1, k), lambda i: (0, 0)),
                  pl.BlockSpec((k, n), lambda i: (0, 0), pipeline_mode=pl.Buffered(1))],
        out_specs=pl.BlockSpec((tm, n), lambda i: (i, 0)),
        compiler_params=_params(1),
        name="mem_kv_proj",
    )(x, g.reshape(1, k), w)


def _in_proj_kernel(rot_lo, rot_hi, x_ref, g_ref, w_ref, s_ref, pos_ref, invf_ref, sign_ref, o_ref,
                    h_ref, cos_ref, sin_ref):
    j = pl.program_id(1)
    tm, tn = o_ref.shape

    assert rot_lo > 0

    @pl.when(j == 0)
    def _():
        chunk = tm // IN_PROJ_ROW_CHUNKS
        for r in range(IN_PROJ_ROW_CHUNKS):
            rows = slice(r * chunk, (r + 1) * chunk)
            h = _rmsnorm_rows(x_ref[rows, :], g_ref[...]).astype(h_ref.dtype)
            h_ref[rows, :] = h
            o_ref[rows, :] = (_dot(h, w_ref[...]) * s_ref[...]).astype(o_ref.dtype)

        pack = HEAD_DIM // ROPE_DIM
        group = tm // pack
        lane = lax.broadcasted_iota(jnp.int32, (group, HEAD_DIM), 1)
        ang = jnp.zeros((group, HEAD_DIM), F32)
        for c in range(pack):
            pos_c = pos_ref[c * group:(c + 1) * group, :].astype(F32)
            in_block = (lane >= c * ROPE_DIM) & (lane < (c + 1) * ROPE_DIM)
            ang = jnp.where(in_block, pos_c * invf_ref[...], ang)
        cos_p = jnp.cos(ang)
        sin_p = jnp.sin(ang) * sign_ref[...]
        for c in range(pack):
            rows = slice(c * group, (c + 1) * group)
            shift = (HEAD_DIM - c * ROPE_DIM) % HEAD_DIM
            cos_c = cos_p if shift == 0 else pltpu.roll(cos_p, shift, axis=1)
            sin_c = sin_p if shift == 0 else pltpu.roll(sin_p, shift, axis=1)
            cos_ref[rows, :] = jnp.where(lane < ROPE_DIM, cos_c, 1.0)
            sin_ref[rows, :] = jnp.where(lane < ROPE_DIM, sin_c, 0.0)

    rotary = (j >= rot_lo) & (j < rot_hi)

    @pl.when((j > 0) & jnp.logical_not(rotary))
    def _():
        o_ref[...] = (_dot(h_ref[...], w_ref[...]) * s_ref[...]).astype(o_ref.dtype)

    @pl.when(rotary)
    def _():
        chunk = tm // IN_PROJ_ROT_CHUNKS
        for r in range(IN_PROJ_ROT_CHUNKS):
            rows = slice(r * chunk, (r + 1) * chunk)
            y = _dot(h_ref[rows, :], w_ref[...]) * s_ref[...]
            cos, sin = cos_ref[rows, :], sin_ref[rows, :]
            for hh in range(tn // HEAD_DIM):
                cols = slice(hh * HEAD_DIM, (hh + 1) * HEAD_DIM)
                o_ref[rows, cols] = _rotate(y[:, cols], cos, sin).astype(o_ref.dtype)


def _in_proj(x, g, w, col_scale, positions, rot_cols):
    m, k = x.shape
    n = w.shape[1]
    tm, tn = MM_TM, MM_TN
    assert rot_cols[0] % tn == 0 and rot_cols[1] % tn == 0
    body = functools.partial(_in_proj_kernel, rot_cols[0] // tn, rot_cols[1] // tn)
    half = ROPE_DIM // 2
    pack = HEAD_DIM // ROPE_DIM
    inv_freq = ROPE_THETA ** (-jnp.arange(0, ROPE_DIM, 2, dtype=F32) / ROPE_DIM)
    invf = jnp.tile(jnp.concatenate([inv_freq, inv_freq]), pack).reshape(1, HEAD_DIM)
    sign = jnp.tile(jnp.concatenate([-jnp.ones((half,), F32), jnp.ones((half,), F32)]),
                    pack).reshape(1, HEAD_DIM)
    row_vec = pl.BlockSpec((1, HEAD_DIM), lambda i, j: (0, 0))
    return pl.pallas_call(
        body,
        out_shape=jax.ShapeDtypeStruct((m, n), BF16),
        grid=(m // tm, n // tn),
        in_specs=[pl.BlockSpec((tm, k), lambda i, j: (i, 0)),
                  pl.BlockSpec((1, k), lambda i, j: (0, 0)),
                  pl.BlockSpec((k, tn), lambda i, j: (0, j)),
                  pl.BlockSpec((1, tn), lambda i, j: (0, j)),
                  pl.BlockSpec((tm, 1), lambda i, j: (i, 0)),
                  row_vec, row_vec],
        out_specs=pl.BlockSpec((tm, tn), lambda i, j: (i, j)),
        scratch_shapes=[pltpu.VMEM((tm, k), BF16),
                        pltpu.VMEM((tm, HEAD_DIM), F32), pltpu.VMEM((tm, HEAD_DIM), F32)],
        compiler_params=_params(2),
        name="in_proj",
    )(x, g.reshape(1, k), w, col_scale.reshape(1, n), positions.reshape(m, 1), invf, sign)


def _mixer_xattn_kernel(osb_ref, omb_ref, wout_ref, x_ref, gx_ref, wq_ref, kv_ref, wo_ref, gf_ref,
                        x2_ref, h3_ref):
    k1 = osb_ref.shape[1]
    w = wq_ref.shape[1]
    half = x_ref.shape[0] // 2
    halves = [slice(r * half, (r + 1) * half) for r in range(2)]

    def out_proj(rows):
        return (x_ref[rows, :] + _dot(osb_ref[rows, :], wout_ref[:k1, :])
                + _dot(omb_ref[rows, :], wout_ref[k1:, :]))

    def q_proj(x1):
        h2 = _rmsnorm_rows(x1, gx_ref[...]).astype(BF16)
        return (_dot(h2, wq_ref[...]) * (ATTN_SCALE * LOG2E)).astype(BF16)

    def attend(q):
        heads = []
        for hh in range(w // HEAD_DIM):
            cols = slice(hh * HEAD_DIM, (hh + 1) * HEAD_DIM)
            s = _dot_nt(q[:, cols], kv_ref[0, :, cols])
            p = jnp.exp2(s - jnp.max(s, axis=-1, keepdims=True))
            l = jnp.sum(p, axis=-1, keepdims=True)
            v = kv_ref[0, :, w + hh * HEAD_DIM: w + (hh + 1) * HEAD_DIM]
            heads.append((_dot(p.astype(BF16), v) / l).astype(BF16))
        return jnp.concatenate(heads, axis=1)

    x1 = [out_proj(rows) for rows in halves]
    q = [q_proj(v) for v in x1]
    ox = [attend(v) for v in q]
    for rows, x1_r, ox_r in zip(halves, x1, ox):
        x2 = x1_r + _dot(ox_r, wo_ref[...])
        x2_ref[rows, :] = x2
        h3_ref[rows, :] = _rmsnorm_rows(x2, gf_ref[...]).astype(h3_ref.dtype)


def _mixer_out_xattn(o_sb, o_mb, w_out, x, g_xattn, w_xq, kv, w_xo, g_ffn, rows_per_batch):
    m, d = x.shape
    k1 = o_sb.shape[1]
    wx = w_xq.shape[1]
    n_mem = kv.shape[1]
    tm = FUSED_TM
    assert rows_per_batch % tm == 0
    tiles_per_batch = rows_per_batch // tm
    const = lambda shape: pl.BlockSpec(shape, lambda i: (0,) * len(shape), pipeline_mode=pl.Buffered(1))
    rows = lambda width: pl.BlockSpec((tm, width), lambda i: (i, 0))
    return pl.pallas_call(
        _mixer_xattn_kernel,
        out_shape=(jax.ShapeDtypeStruct((m, d), F32), jax.ShapeDtypeStruct((m, d), BF16)),
        grid=(m // tm,),
        in_specs=[rows(k1), rows(k1), const((2 * k1, d)), rows(d), const((1, d)), const((d, wx)),
                  pl.BlockSpec((1, n_mem, 2 * wx), lambda i: (i // tiles_per_batch, 0, 0)),
                  const((wx, d)), const((1, d))],
        out_specs=(rows(d), rows(d)),
        compiler_params=_params(1),
        name="mixer_out_xattn",
    )(o_sb, o_mb, w_out, x, g_xattn.reshape(1, d), w_xq, kv, w_xo, g_ffn.reshape(1, d))


def _down_final_kernel(a_ref, w_ref, r_ref, g_ref, o_ref):
    chunk = a_ref.shape[0] // DOWN_ROW_CHUNKS
    for r in range(DOWN_ROW_CHUNKS):
        rows = slice(r * chunk, (r + 1) * chunk)
        x3 = r_ref[rows, :] + _dot(a_ref[rows, :], w_ref[...])
        o_ref[rows, :] = _rmsnorm_rows(x3, g_ref[...])


def _down_final(a, w, residual, g):
    m, k = a.shape
    n = w.shape[1]
    tm = DOWN_TM
    return pl.pallas_call(
        _down_final_kernel,
        out_shape=jax.ShapeDtypeStruct((m, n), F32),
        grid=(m // tm,),
        in_specs=[pl.BlockSpec((tm, k), lambda i: (i, 0)),
                  pl.BlockSpec((k, n), lambda i: (0, 0), pipeline_mode=pl.Buffered(1)),
                  pl.BlockSpec((tm, n), lambda i: (i, 0)),
                  pl.BlockSpec((1, n), lambda i: (0, 0))],
        out_specs=pl.BlockSpec((tm, n), lambda i: (i, 0)),
        compiler_params=_params(1),
        name="down_final",
    )(a, w, residual, g.reshape(1, n))


def _swiglu_kernel(a_ref, wg_ref, wu_ref, o_ref, w_bf):
    tn = wg_ref.shape[1]

    @pl.when(pl.program_id(1) == 0)
    def _():
        w_bf[:, :tn] = wg_ref[...].astype(BF16)
        w_bf[:, tn:] = wu_ref[...].astype(BF16)

    chunk = a_ref.shape[0] // FFN_ROW_CHUNKS
    for r in range(FFN_ROW_CHUNKS):
        rows = slice(r * chunk, (r + 1) * chunk)
        gu = _dot(a_ref[rows, :], w_bf[...])
        g, u = gu[:, :tn], gu[:, tn:]
        o_ref[rows, :] = (g / (1.0 + jnp.exp(-g)) * u).astype(o_ref.dtype)


def _swiglu_up(h, w_gate_up):
    m, k = h.shape
    d_ff = w_gate_up.shape[1] // 2
    tm, tn = FFN_TM, FFN_TN
    nj = d_ff // tn
    return pl.pallas_call(
        _swiglu_kernel,
        out_shape=jax.ShapeDtypeStruct((m, d_ff), BF16),
        grid=(nj, m // tm),
        in_specs=[pl.BlockSpec((tm, k), lambda j, i: (i, 0)),
                  pl.BlockSpec((k, tn), lambda j, i: (0, j)),
                  pl.BlockSpec((k, tn), lambda j, i: (0, j + nj))],
        out_specs=pl.BlockSpec((tm, tn), lambda j, i: (i, j)),
        scratch_shapes=[pltpu.VMEM((k, 2 * tn), BF16)],
        compiler_params=_params(2),
        name="swiglu_up",
    )(h, w_gate_up, w_gate_up)


def _rotate(x, cos, sin_signed):
    half = ROPE_DIM // 2
    lane = lax.broadcasted_iota(jnp.int32, x.shape, 1)
    partner = jnp.where(lane < half,
                        pltpu.roll(x, HEAD_DIM - half, axis=1),
                        pltpu.roll(x, half, axis=1))
    return x * cos + partner * sin_signed


def _sb_kernel(q_ref, k_ref, v_ref, g_ref, o_ref, kt_ref):
    t = ATT_TQ
    d = HEAD_DIM
    n_tiles = k_ref.shape[1] // t
    heads = k_ref.shape[2] // d
    row = lax.broadcasted_iota(jnp.int32, (t, t), 0)
    col = lax.broadcasted_iota(jnp.int32, (t, t), 1)
    past = col < row
    neg_from = jnp.where(row >= col, -1.0, 0.0).astype(BF16)

    for j in range(n_tiles):
        for hh in range(heads):
            kt_ref[hh, :, j * t:(j + 1) * t] = (
                k_ref[0, j * t:(j + 1) * t, hh * d:(hh + 1) * d].astype(F32).T.astype(BF16))

    def logits(hh, i):
        q = q_ref[0, i * t:(i + 1) * t, hh * d:(hh + 1) * d]
        z, sp = [], []
        for j in range(i + 1):
            z_j = _dot(q, kt_ref[hh, :, j * t:(j + 1) * t])
            sp_j = jnp.maximum(z_j, 0.0) + jnp.log2(1.0 + jnp.exp2(-jnp.abs(z_j)))
            if j == i:
                sp_j = jnp.where(past, sp_j, 0.0)
            z.append(z_j)
            sp.append(sp_j.astype(BF16))
        return z, sp

    def weigh(hh, i, z, sp):
        cols = slice(hh * d, (hh + 1) * d)
        carry = jnp.zeros((t, 1), F32)
        acc = None
        for j in reversed(range(i + 1)):
            tail = _dot(sp[j], neg_from)
            w_j = jnp.exp2(z[j] + tail + carry)
            if j == i:
                w_j = jnp.where(past, w_j, 0.0)
            pv = _dot(w_j.astype(BF16), v_ref[0, j * t:(j + 1) * t, cols])
            acc = pv if acc is None else acc + pv
            carry = carry + tail[:, 0:1]
        o_ref[0, i * t:(i + 1) * t, cols] = _rmsnorm_rows(acc, g_ref[:, cols]).astype(o_ref.dtype)

    items = [(hh, i if hh % 2 == 0 else n_tiles - 1 - i) for i in range(n_tiles) for hh in range(heads)]
    pending = None
    for item in items:
        z, sp = logits(*item)
        if pending is not None:
            weigh(*pending)
        pending = (*item, z, sp)
    weigh(*pending)


def _sb_attention(qkv, g_out):
    b, s, _ = qkv.shape
    h = N_HEADS_SB
    hps = SB_HEADS_PER_STEP
    groups = h // hps
    width = hps * HEAD_DIM
    blk = lambda base: pl.BlockSpec((1, s, width), lambda bi, gi: (bi, 0, base + gi))
    return pl.pallas_call(
        _sb_kernel,
        out_shape=jax.ShapeDtypeStruct((b, s, h * HEAD_DIM), BF16),
        grid=(b, groups),
        in_specs=[blk(0), blk(groups), blk(2 * groups),
                  pl.BlockSpec((1, width), lambda bi, gi: (0, gi))],
        out_specs=blk(0),
        scratch_shapes=[pltpu.VMEM((hps, HEAD_DIM, s), BF16)],
        compiler_params=_params(2),
        name="sb_attention",
    )(qkv, qkv, qkv, g_out.reshape(1, h * HEAD_DIM))


def _moba_kernel(q_ref, k_ref, v_ref, g_ref, o_ref, ka_ref, km_ref):
    blk = MOBA_BLOCK
    nb = k_ref.shape[1] // blk
    d = HEAD_DIM
    heads = k_ref.shape[2] // d
    row = lax.broadcasted_iota(jnp.int32, (blk, blk), 0)
    col = lax.broadcasted_iota(jnp.int32, (blk, blk), 1)
    causal = col <= row
    sub = lax.broadcasted_iota(jnp.int32, (d, blk), 0)
    blk_id = lax.broadcasted_iota(jnp.int32, (km_ref.shape[1], blk), 0)

    km_ref[...] = jnp.zeros_like(km_ref)
    for j in range(nb):
        rows = slice(j * blk, (j + 1) * blk)
        for hh in range(heads):
            cols = slice(hh * d, (hh + 1) * d)
            kr = k_ref[0, rows, cols].astype(F32)
            ka_ref[hh, :d, rows] = kr.T.astype(BF16)
            ka_ref[hh, d:, rows] = jnp.where(sub == j, 1.0, 0.0).astype(BF16)
            km_ref[hh, j:j + 1, :] = jnp.mean(kr, axis=0, keepdims=True)

    def scores(hh, n):
        cols = slice(hh * d, (hh + 1) * d)
        rows = slice(n * blk, (n + 1) * blk)
        keys = (n + 1) * blk
        q = q_ref[0, rows, cols]
        if n <= MOBA_TOP_K:
            lhs, k_rows = q, slice(0, d)
        else:
            km = km_ref[hh]
            km_hi = km.astype(BF16)
            km_lo = (km - km_hi.astype(F32)).astype(BF16)
            gate = _dot_nt(km_hi, q) + _dot_nt(km_lo, q)
            rank = jnp.zeros(gate.shape, F32)
            for c in range(n):
                gc = gate[c:c + 1, :]
                beats = (gc > gate) | ((gc == gate) & (c < blk_id))
                rank = rank + jnp.where(beats, 1.0, 0.0)
            allowed = ((rank < MOBA_TOP_K) & (blk_id < n)) | (blk_id == n)
            bias_t = jnp.where(allowed, 0.0, NEG_BIG)
            bias_t = jnp.concatenate([bias_t, jnp.zeros((d - bias_t.shape[0], blk), F32)], axis=0)
            lhs = jnp.concatenate([q, bias_t.T.astype(BF16)], axis=1)
            k_rows = slice(0, 2 * d)
        s = []
        m_elem = None
        for j in range(n + 1):
            s_j = _dot(lhs, ka_ref[hh, k_rows, j * blk:(j + 1) * blk])
            if j == n:
                s_j = jnp.where(causal, s_j, -jnp.inf)
            s.append(s_j)
            m_elem = s_j if m_elem is None else jnp.maximum(m_elem, s_j)
        return s, jnp.max(m_elem, axis=-1, keepdims=True)

    def attend(hh, n, s, m):
        cols = slice(hh * d, (hh + 1) * d)
        rows = slice(n * blk, (n + 1) * blk)
        l_elem = None
        acc = None
        for j in range(n + 1):
            p_j = jnp.exp2(s[j] - m)
            pv = _dot(p_j.astype(BF16), v_ref[0, j * blk:(j + 1) * blk, cols])
            l_elem = p_j if l_elem is None else l_elem + p_j
            acc = pv if acc is None else acc + pv
        l = jnp.sum(l_elem, axis=-1, keepdims=True)
        o_ref[0, rows, cols] = _rmsnorm_rows(acc / l, g_ref[:, cols]).astype(o_ref.dtype)

    items = [(hh, n if hh % 2 == 0 else nb - 1 - n) for n in range(nb) for hh in range(heads)]
    pending = None
    for item in items:
        s, m = scores(*item)
        if pending is not None:
            attend(*pending)
        pending = (*item, s, m)
    attend(*pending)


def _moba_attention(qkv, g_out):
    b, s, _ = qkv.shape
    h = N_HEADS_MOBA
    hps = MOBA_HEADS_PER_STEP
    groups = h // hps
    width = hps * HEAD_DIM
    base = 3 * N_HEADS_SB // hps
    km_rows = 16
    assert s % MOBA_BLOCK == 0 and s // MOBA_BLOCK <= km_rows
    blk = lambda off: pl.BlockSpec((1, s, width), lambda bi, gi: (bi, 0, off + gi))
    return pl.pallas_call(
        _moba_kernel,
        out_shape=jax.ShapeDtypeStruct((b, s, h * HEAD_DIM), BF16),
        grid=(b, groups),
        in_specs=[blk(base), blk(base + groups), blk(base + 2 * groups),
                  pl.BlockSpec((1, width), lambda bi, gi: (0, gi))],
        out_specs=blk(0),
        scratch_shapes=[pltpu.VMEM((hps, 2 * HEAD_DIM, s), BF16),
                        pltpu.VMEM((hps, km_rows, HEAD_DIM), F32)],
        compiler_params=_params(2),
        name="moba_attention",
    )(qkv, qkv, qkv, g_out.reshape(1, h * HEAD_DIM))


def kernel(x, mem, positions, g_mix, w_in, g_out_sb, g_out_moba, w_out, g_xattn, g_mem,
           w_xq, w_xkv, w_xo, g_ffn, w_gate_up, w_down, g_final):
    b, s, d = x.shape
    n_mem = mem.shape[1]
    n = b * s
    x0 = x.reshape(n, d)

    w_sb, w_mb = N_HEADS_SB * HEAD_DIM, N_HEADS_MOBA * HEAD_DIM
    q_scale = jnp.concatenate([jnp.full((w_sb,), ATTN_SCALE * LOG2E, F32), jnp.ones((2 * w_sb,), F32),
                               jnp.full((w_mb,), ATTN_SCALE * LOG2E, F32), jnp.ones((2 * w_mb,), F32)])
    rot_cols = (3 * w_sb, 3 * w_sb + 2 * w_mb)
    qkv = _in_proj(x0, g_mix, w_in.astype(BF16), q_scale, positions, rot_cols).reshape(b, s, -1)
    o_sb = _sb_attention(qkv, g_out_sb)
    o_mb = _moba_attention(qkv, g_out_moba)

    kv = _norm_proj(mem.reshape(b * n_mem, d), g_mem, w_xkv).reshape(b, n_mem, -1)

    x2, h = _mixer_out_xattn(o_sb.reshape(n, -1), o_mb.reshape(n, -1), w_out.astype(BF16), x0,
                             g_xattn, w_xq.astype(BF16), kv, w_xo.astype(BF16), g_ffn, s)

    a = _swiglu_up(h, w_gate_up)
    return _down_final(a, w_down.astype(BF16), x2, g_final).reshape(b, s, d)
```

```python
import functools

import jax
import jax.numpy as jnp
from jax import lax
from jax.experimental import pallas as pl
from jax.experimental.pallas import tpu as pltpu

F32 = jnp.float32
BF16 = jnp.bfloat16

HEAD_DIM = 128
N_HEADS_SB = 8
N_HEADS_MOBA = 8
N_HEADS_X = 4
MOBA_BLOCK = 256
MOBA_TOP_K = 3
ROPE_DIM = HEAD_DIM // 4
ROPE_THETA = 500000.0
EPS = 1e-6
ATTN_SCALE = HEAD_DIM ** -0.5
LOG2E = 1.4426950408889634

V7X_VMEM_BYTES = 64 * 1024 * 1024
VMEM_LIMIT_BYTES = V7X_VMEM_BYTES * 7 // 8

MM_TM = 1024
MM_TN = 1024
IN_PROJ_ROW_CHUNKS = 4
IN_PROJ_ROT_CHUNKS = 4
FFN_TM = 2048
FFN_TN = 512
FFN_ROW_CHUNKS = 8
FUSED_TM = 512
DOWN_TM = 512
DOWN_ROW_CHUNKS = 2
ATT_TQ = 256
SB_HEADS_PER_STEP = 2
MOBA_HEADS_PER_STEP = 2
NEG_BIG = -1e30

_NT = (((1,), (1,)), ((), ()))


def _params(n_axes):
    return pltpu.CompilerParams(
        dimension_semantics=("arbitrary",) * n_axes,
        vmem_limit_bytes=VMEM_LIMIT_BYTES)


def _dot(a, b):
    return jnp.dot(a, b, preferred_element_type=F32)


def _dot_nt(a, b):
    return lax.dot_general(a, b, _NT, preferred_element_type=F32)


def _rmsnorm_rows(x, g):
    ms = jnp.mean(x * x, axis=-1, keepdims=True)
    return x * lax.rsqrt(ms + EPS) * g


def _norm_proj_kernel(x_ref, g_ref, w_ref, o_ref):
    h = _rmsnorm_rows(x_ref[...], g_ref[...]).astype(BF16)
    o_ref[...] = _dot(h, w_ref[...].astype(BF16)).astype(o_ref.dtype)


def _norm_proj(x, g, w):
    m, k = x.shape
    n = w.shape[1]
    tm = min(MM_TM, m)
    return pl.pallas_call(
        _norm_proj_kernel,
        out_shape=jax.ShapeDtypeStruct((m, n), BF16),
        grid=(m // tm,),
        in_specs=[pl.BlockSpec((tm, k), lambda i: (i, 0)),
                  pl.BlockSpec((1, k), lambda i: (0, 0)),
                  pl.BlockSpec((k, n), lambda i: (0, 0), pipeline_mode=pl.Buffered(1))],
        out_specs=pl.BlockSpec((tm, n), lambda i: (i, 0)),
        compiler_params=_params(1),
        name="mem_kv_proj",
    )(x, g.reshape(1, k), w)


def _in_proj_kernel(rot_lo, rot_hi, x_ref, g_ref, w_ref, s_ref, pos_ref, invf_ref, sign_ref, o_ref,
                    h_ref, cos_ref, sin_ref):
    j = pl.program_id(1)
    tm, tn = o_ref.shape

    assert rot_lo > 0

    @pl.when(j == 0)
    def _():
        chunk = tm // IN_PROJ_ROW_CHUNKS
        for r in range(IN_PROJ_ROW_CHUNKS):
            rows = slice(r * chunk, (r + 1) * chunk)
            h = _rmsnorm_rows(x_ref[rows, :], g_ref[...]).astype(h_ref.dtype)
            h_ref[rows, :] = h
            o_ref[rows, :] = (_dot(h, w_ref[...]) * s_ref[...]).astype(o_ref.dtype)

        pack = HEAD_DIM // ROPE_DIM
        group = tm // pack
        lane = lax.broadcasted_iota(jnp.int32, (group, HEAD_DIM), 1)
        ang = jnp.zeros((group, HEAD_DIM), F32)
        for c in range(pack):
            pos_c = pos_ref[c * group:(c + 1) * group, :].astype(F32)
            in_block = (lane >= c * ROPE_DIM) & (lane < (c + 1) * ROPE_DIM)
            ang = jnp.where(in_block, pos_c * invf_ref[...], ang)
        cos_p = jnp.cos(ang)
        sin_p = jnp.sin(ang) * sign_ref[...]
        for c in range(pack):
            rows = slice(c * group, (c + 1) * group)
            shift = (HEAD_DIM - c * ROPE_DIM) % HEAD_DIM
            cos_c = cos_p if shift == 0 else pltpu.roll(cos_p, shift, axis=1)
            sin_c = sin_p if shift == 0 else pltpu.roll(sin_p, shift, axis=1)
            cos_ref[rows, :] = jnp.where(lane < ROPE_DIM, cos_c, 1.0)
            sin_ref[rows, :] = jnp.where(lane < ROPE_DIM, sin_c, 0.0)

    rotary = (j >= rot_lo) & (j < rot_hi)

    @pl.when((j > 0) & jnp.logical_not(rotary))
    def _():
        o_ref[...] = (_dot(h_ref[...], w_ref[...]) * s_ref[...]).astype(o_ref.dtype)

    @pl.when(rotary)
    def _():
        chunk = tm // IN_PROJ_ROT_CHUNKS
        for r in range(IN_PROJ_ROT_CHUNKS):
            rows = slice(r * chunk, (r + 1) * chunk)
            y = _dot(h_ref[rows, :], w_ref[...]) * s_ref[...]
            cos, sin = cos_ref[rows, :], sin_ref[rows, :]
            for hh in range(tn // HEAD_DIM):
                cols = slice(hh * HEAD_DIM, (hh + 1) * HEAD_DIM)
                o_ref[rows, cols] = _rotate(y[:, cols], cos, sin).astype(o_ref.dtype)


def _in_proj(x, g, w, col_scale, positions, rot_cols):
    m, k = x.shape
    n = w.shape[1]
    tm, tn = MM_TM, MM_TN
    assert rot_cols[0] % tn == 0 and rot_cols[1] % tn == 0
    body = functools.partial(_in_proj_kernel, rot_cols[0] // tn, rot_cols[1] // tn)
    half = ROPE_DIM // 2
    pack = HEAD_DIM // ROPE_DIM
    inv_freq = ROPE_THETA ** (-jnp.arange(0, ROPE_DIM, 2, dtype=F32) / ROPE_DIM)
    invf = jnp.tile(jnp.concatenate([inv_freq, inv_freq]), pack).reshape(1, HEAD_DIM)
    sign = jnp.tile(jnp.concatenate([-jnp.ones((half,), F32), jnp.ones((half,), F32)]),
                    pack).reshape(1, HEAD_DIM)
    row_vec = pl.BlockSpec((1, HEAD_DIM), lambda i, j: (0, 0))
    return pl.pallas_call(
        body,
        out_shape=jax.ShapeDtypeStruct((m, n), BF16),
        grid=(m // tm, n // tn),
        in_specs=[pl.BlockSpec((tm, k), lambda i, j: (i, 0)),
                  pl.BlockSpec((1, k), lambda i, j: (0, 0)),
                  pl.BlockSpec((k, tn), lambda i, j: (0, j)),
                  pl.BlockSpec((1, tn), lambda i, j: (0, j)),
                  pl.BlockSpec((tm, 1), lambda i, j: (i, 0)),
                  row_vec, row_vec],
        out_specs=pl.BlockSpec((tm, tn), lambda i, j: (i, j)),
        scratch_shapes=[pltpu.VMEM((tm, k), BF16),
                        pltpu.VMEM((tm, HEAD_DIM), F32), pltpu.VMEM((tm, HEAD_DIM), F32)],
        compiler_params=_params(2),
        name="in_proj",
    )(x, g.reshape(1, k), w, col_scale.reshape(1, n), positions.reshape(m, 1), invf, sign)


def _mixer_xattn_kernel(osb_ref, omb_ref, wout_ref, x_ref, gx_ref, wq_ref, kv_ref, wo_ref, gf_ref,
                        x2_ref, h3_ref):
    k1 = osb_ref.shape[1]
    w = wq_ref.shape[1]
    half = x_ref.shape[0] // 2
    halves = [slice(r * half, (r + 1) * half) for r in range(2)]

    def out_proj(rows):
        return (x_ref[rows, :] + _dot(osb_ref[rows, :], wout_ref[:k1, :])
                + _dot(omb_ref[rows, :], wout_ref[k1:, :]))

    def q_proj(x1):
        h2 = _rmsnorm_rows(x1, gx_ref[...]).astype(BF16)
        return (_dot(h2, wq_ref[...]) * (ATTN_SCALE * LOG2E)).astype(BF16)

    def attend(q):
        heads = []
        for hh in range(w // HEAD_DIM):
            cols = slice(hh * HEAD_DIM, (hh + 1) * HEAD_DIM)
            s = _dot_nt(q[:, cols], kv_ref[0, :, cols])
            p = jnp.exp2(s - jnp.max(s, axis=-1, keepdims=True))
            l = jnp.sum(p, axis=-1, keepdims=True)
            v = kv_ref[0, :, w + hh * HEAD_DIM: w + (hh + 1) * HEAD_DIM]
            heads.append((_dot(p.astype(BF16), v) / l).astype(BF16))
        return jnp.concatenate(heads, axis=1)

    x1 = [out_proj(rows) for rows in halves]
    q = [q_proj(v) for v in x1]
    ox = [attend(v) for v in q]
    for rows, x1_r, ox_r in zip(halves, x1, ox):
        x2 = x1_r + _dot(ox_r, wo_ref[...])
        x2_ref[rows, :] = x2
        h3_ref[rows, :] = _rmsnorm_rows(x2, gf_ref[...]).astype(h3_ref.dtype)


def _mixer_out_xattn(o_sb, o_mb, w_out, x, g_xattn, w_xq, kv, w_xo, g_ffn, rows_per_batch):
    m, d = x.shape
    k1 = o_sb.shape[1]
    wx = w_xq.shape[1]
    n_mem = kv.shape[1]
    tm = FUSED_TM
    assert rows_per_batch % tm == 0
    tiles_per_batch = rows_per_batch // tm
    const = lambda shape: pl.BlockSpec(shape, lambda i: (0,) * len(shape), pipeline_mode=pl.Buffered(1))
    rows = lambda width: pl.BlockSpec((tm, width), lambda i: (i, 0))
    return pl.pallas_call(
        _mixer_xattn_kernel,
        out_shape=(jax.ShapeDtypeStruct((m, d), F32), jax.ShapeDtypeStruct((m, d), BF16)),
        grid=(m // tm,),
        in_specs=[rows(k1), rows(k1), const((2 * k1, d)), rows(d), const((1, d)), const((d, wx)),
                  pl.BlockSpec((1, n_mem, 2 * wx), lambda i: (i // tiles_per_batch, 0, 0)),
                  const((wx, d)), const((1, d))],
        out_specs=(rows(d), rows(d)),
        compiler_params=_params(1),
        name="mixer_out_xattn",
    )(o_sb, o_mb, w_out, x, g_xattn.reshape(1, d), w_xq, kv, w_xo, g_ffn.reshape(1, d))


def _down_final_kernel(a_ref, w_ref, r_ref, g_ref, o_ref):
    chunk = a_ref.shape[0] // DOWN_ROW_CHUNKS
    for r in range(DOWN_ROW_CHUNKS):
        rows = slice(r * chunk, (r + 1) * chunk)
        x3 = r_ref[rows, :] + _dot(a_ref[rows, :], w_ref[...])
        o_ref[rows, :] = _rmsnorm_rows(x3, g_ref[...])


def _down_final(a, w, residual, g):
    m, k = a.shape
    n = w.shape[1]
    tm = DOWN_TM
    return pl.pallas_call(
        _down_final_kernel,
        out_shape=jax.ShapeDtypeStruct((m, n), F32),
        grid=(m // tm,),
        in_specs=[pl.BlockSpec((tm, k), lambda i: (i, 0)),
                  pl.BlockSpec((k, n), lambda i: (0, 0), pipeline_mode=pl.Buffered(1)),
                  pl.BlockSpec((tm, n), lambda i: (i, 0)),
                  pl.BlockSpec((1, n), lambda i: (0, 0))],
        out_specs=pl.BlockSpec((tm, n), lambda i: (i, 0)),
        compiler_params=_params(1),
        name="down_final",
    )(a, w, residual, g.reshape(1, n))


def _swiglu_kernel(a_ref, wg_ref, wu_ref, o_ref, w_bf):
    tn = wg_ref.shape[1]

    @pl.when(pl.program_id(1) == 0)
    def _():
        w_bf[:, :tn] = wg_ref[...].astype(BF16)
        w_bf[:, tn:] = wu_ref[...].astype(BF16)

    chunk = a_ref.shape[0] // FFN_ROW_CHUNKS
    for r in range(FFN_ROW_CHUNKS):
        rows = slice(r * chunk, (r + 1) * chunk)
        gu = _dot(a_ref[rows, :], w_bf[...])
        g, u = gu[:, :tn], gu[:, tn:]
        o_ref[rows, :] = (g / (1.0 + jnp.exp(-g)) * u).astype(o_ref.dtype)


def _swiglu_up(h, w_gate_up):
    m, k = h.shape
    d_ff = w_gate_up.shape[1] // 2
    tm, tn = FFN_TM, FFN_TN
    nj = d_ff // tn
    return pl.pallas_call(
        _swiglu_kernel,
        out_shape=jax.ShapeDtypeStruct((m, d_ff), BF16),
        grid=(nj, m // tm),
        in_specs=[pl.BlockSpec((tm, k), lambda j, i: (i, 0)),
                  pl.BlockSpec((k, tn), lambda j, i: (0, j)),
                  pl.BlockSpec((k, tn), lambda j, i: (0, j + nj))],
        out_specs=pl.BlockSpec((tm, tn), lambda j, i: (i, j)),
        scratch_shapes=[pltpu.VMEM((k, 2 * tn), BF16)],
        compiler_params=_params(2),
        name="swiglu_up",
    )(h, w_gate_up, w_gate_up)


def _rotate(x, cos, sin_signed):
    half = ROPE_DIM // 2
    lane = lax.broadcasted_iota(jnp.int32, x.shape, 1)
    partner = jnp.where(lane < half,
                        pltpu.roll(x, HEAD_DIM - half, axis=1),
                        pltpu.roll(x, half, axis=1))
    return x * cos + partner * sin_signed


def _sb_kernel(q_ref, k_ref, v_ref, g_ref, o_ref, kt_ref):
    t = ATT_TQ
    d = HEAD_DIM
    n_tiles = k_ref.shape[1] // t
    heads = k_ref.shape[2] // d
    row = lax.broadcasted_iota(jnp.int32, (t, t), 0)
    col = lax.broadcasted_iota(jnp.int32, (t, t), 1)
    past = col < row
    neg_from = jnp.where(row >= col, -1.0, 0.0).astype(BF16)

    for j in range(n_tiles):
        for hh in range(heads):
            kt_ref[hh, :, j * t:(j + 1) * t] = (
                k_ref[0, j * t:(j + 1) * t, hh * d:(hh + 1) * d].astype(F32).T.astype(BF16))

    def logits(hh, i):
        q = q_ref[0, i * t:(i + 1) * t, hh * d:(hh + 1) * d]
        z, sp = [], []
        for j in range(i + 1):
            z_j = _dot(q, kt_ref[hh, :, j * t:(j + 1) * t])
            sp_j = jnp.maximum(z_j, 0.0) + jnp.log2(1.0 + jnp.exp2(-jnp.abs(z_j)))
            if j == i:
                sp_j = jnp.where(past, sp_j, 0.0)
            z.append(z_j)
            sp.append(sp_j.astype(BF16))
        return z, sp

    def weigh(hh, i, z, sp):
        cols = slice(hh * d, (hh + 1) * d)
        carry = jnp.zeros((t, 1), F32)
        acc = None
        for j in reversed(range(i + 1)):
            tail = _dot(sp[j], neg_from)
            w_j = jnp.exp2(z[j] + tail + carry)
            if j == i:
                w_j = jnp.where(past, w_j, 0.0)
            pv = _dot(w_j.astype(BF16), v_ref[0, j * t:(j + 1) * t, cols])
            acc = pv if acc is None else acc + pv
            carry = carry + tail[:, 0:1]
        o_ref[0, i * t:(i + 1) * t, cols] = _rmsnorm_rows(acc, g_ref[:, cols]).astype(o_ref.dtype)

    items = [(hh, i if hh % 2 == 0 else n_tiles - 1 - i) for i in range(n_tiles) for hh in range(heads)]
    pending = None
    for item in items:
        z, sp = logits(*item)
        if pending is not None:
            weigh(*pending)
        pending = (*item, z, sp)
    weigh(*pending)


def _sb_attention(qkv, g_out):
    b, s, _ = qkv.shape
    h = N_HEADS_SB
    hps = SB_HEADS_PER_STEP
    groups = h // hps
    width = hps * HEAD_DIM
    blk = lambda base: pl.BlockSpec((1, s, width), lambda bi, gi: (bi, 0, base + gi))
    return pl.pallas_call(
        _sb_kernel,
        out_shape=jax.ShapeDtypeStruct((b, s, h * HEAD_DIM), BF16),
        grid=(b, groups),
        in_specs=[blk(0), blk(groups), blk(2 * groups),
                  pl.BlockSpec((1, width), lambda bi, gi: (0, gi))],
        out_specs=blk(0),
        scratch_shapes=[pltpu.VMEM((hps, HEAD_DIM, s), BF16)],
        compiler_params=_params(2),
        name="sb_attention",
    )(qkv, qkv, qkv, g_out.reshape(1, h * HEAD_DIM))


def _moba_kernel(q_ref, k_ref, v_ref, g_ref, o_ref, ka_ref, km_ref):
    blk = MOBA_BLOCK
    nb = k_ref.shape[1] // blk
    d = HEAD_DIM
    heads = k_ref.shape[2] // d
    row = lax.broadcasted_iota(jnp.int32, (blk, blk), 0)
    col = lax.broadcasted_iota(jnp.int32, (blk, blk), 1)
    causal = col <= row
    sub = lax.broadcasted_iota(jnp.int32, (d, blk), 0)
    blk_id = lax.broadcasted_iota(jnp.int32, (km_ref.shape[1], blk), 0)

    km_ref[...] = jnp.zeros_like(km_ref)
    for j in range(nb):
        rows = slice(j * blk, (j + 1) * blk)
        for hh in range(heads):
            cols = slice(hh * d, (hh + 1) * d)
            kr = k_ref[0, rows, cols].astype(F32)
            ka_ref[hh, :d, rows] = kr.T.astype(BF16)
            ka_ref[hh, d:, rows] = jnp.where(sub == j, 1.0, 0.0).astype(BF16)
            km_ref[hh, j:j + 1, :] = jnp.mean(kr, axis=0, keepdims=True)

    def scores(hh, n):
        cols = slice(hh * d, (hh + 1) * d)
        rows = slice(n * blk, (n + 1) * blk)
        keys = (n + 1) * blk
        q = q_ref[0, rows, cols]
        if n <= MOBA_TOP_K:
            lhs, k_rows = q, slice(0, d)
        else:
            km = km_ref[hh]
            km_hi = km.astype(BF16)
            km_lo = (km - km_hi.astype(F32)).astype(BF16)
            gate = _dot_nt(km_hi, q) + _dot_nt(km_lo, q)
            rank = jnp.zeros(gate.shape, F32)
            for c in range(n):
                gc = gate[c:c + 1, :]
                beats = (gc > gate) | ((gc == gate) & (c < blk_id))
                rank = rank + jnp.where(beats, 1.0, 0.0)
            allowed = ((rank < MOBA_TOP_K) & (blk_id < n)) | (blk_id == n)
            bias_t = jnp.where(allowed, 0.0, NEG_BIG)
            bias_t = jnp.concatenate([bias_t, jnp.zeros((d - bias_t.shape[0], blk), F32)], axis=0)
            lhs = jnp.concatenate([q, bias_t.T.astype(BF16)], axis=1)
            k_rows = slice(0, 2 * d)
        s = []
        m_elem = None
        for j in range(n + 1):
            s_j = _dot(lhs, ka_ref[hh, k_rows, j * blk:(j + 1) * blk])
            if j == n:
                s_j = jnp.where(causal, s_j, -jnp.inf)
            s.append(s_j)
            m_elem = s_j if m_elem is None else jnp.maximum(m_elem, s_j)
        return s, jnp.max(m_elem, axis=-1, keepdims=True)

    def attend(hh, n, s, m):
        cols = slice(hh * d, (hh + 1) * d)
        rows = slice(n * blk, (n + 1) * blk)
        l_elem = None
        acc = None
        for j in range(n + 1):
            p_j = jnp.exp2(s[j] - m)
            pv = _dot(p_j.astype(BF16), v_ref[0, j * blk:(j + 1) * blk, cols])
            l_elem = p_j if l_elem is None else l_elem + p_j
            acc = pv if acc is None else acc + pv
        l = jnp.sum(l_elem, axis=-1, keepdims=True)
        o_ref[0, rows, cols] = _rmsnorm_rows(acc / l, g_ref[:, cols]).astype(o_ref.dtype)

    items = [(hh, n if hh % 2 == 0 else nb - 1 - n) for n in range(nb) for hh in range(heads)]
    pending = None
    for item in items:
        s, m = scores(*item)
        if pending is not None:
            attend(*pending)
        pending = (*item, s, m)
    attend(*pending)


def _moba_attention(qkv, g_out):
    b, s, _ = qkv.shape
    h = N_HEADS_MOBA
    hps = MOBA_HEADS_PER_STEP
    groups = h // hps
    width = hps * HEAD_DIM
    base = 3 * N_HEADS_SB // hps
    km_rows = 16
    assert s % MOBA_BLOCK == 0 and s // MOBA_BLOCK <= km_rows
    blk = lambda off: pl.BlockSpec((1, s, width), lambda bi, gi: (bi, 0, off + gi))
    return pl.pallas_call(
        _moba_kernel,
        out_shape=jax.ShapeDtypeStruct((b, s, h * HEAD_DIM), BF16),
        grid=(b, groups),
        in_specs=[blk(base), blk(base + groups), blk(base + 2 * groups),
                  pl.BlockSpec((1, width), lambda bi, gi: (0, gi))],
        out_specs=blk(0),
        scratch_shapes=[pltpu.VMEM((hps, 2 * HEAD_DIM, s), BF16),
                        pltpu.VMEM((hps, km_rows, HEAD_DIM), F32)],
        compiler_params=_params(2),
        name="moba_attention",
    )(qkv, qkv, qkv, g_out.reshape(1, h * HEAD_DIM))


def kernel(x, mem, positions, g_mix, w_in, g_out_sb, g_out_moba, w_out, g_xattn, g_mem,
           w_xq, w_xkv, w_xo, g_ffn, w_gate_up, w_down, g_final):
    b, s, d = x.shape
    n_mem = mem.shape[1]
    n = b * s
    x0 = x.reshape(n, d)

    w_sb, w_mb = N_HEADS_SB * HEAD_DIM, N_HEADS_MOBA * HEAD_DIM
    q_scale = jnp.concatenate([jnp.full((w_sb,), ATTN_SCALE * LOG2E, F32), jnp.ones((2 * w_sb,), F32),
                               jnp.full((w_mb,), ATTN_SCALE * LOG2E, F32), jnp.ones((2 * w_mb,), F32)])
    rot_cols = (3 * w_sb, 3 * w_sb + 2 * w_mb)
    qkv = _in_proj(x0, g_mix, w_in.astype(BF16), q_scale, positions, rot_cols).reshape(b, s, -1)
    o_sb = _sb_attention(qkv, g_out_sb)
    o_mb = _moba_attention(qkv, g_out_moba)

    kv = _norm_proj(mem.reshape(b * n_mem, d), g_mem, w_xkv).reshape(b, n_mem, -1)

    x2, h = _mixer_out_xattn(o_sb.reshape(n, -1), o_mb.reshape(n, -1), w_out.astype(BF16), x0,
                             g_xattn, w_xq.astype(BF16), kv, w_xo.astype(BF16), g_ffn, s)

    a = _swiglu_up(h, w_gate_up)
    return _down_final(a, w_down.astype(BF16), x2, g_final).reshape(b, s, d)
```

```python
import functools

import jax
import jax.numpy as jnp
from jax import lax
from jax.experimental import pallas as pl
from jax.experimental.pallas import tpu as pltpu

F32 = jnp.float32
BF16 = jnp.bfloat16

HEAD_DIM = 128
N_HEADS_SB = 8
N_HEADS_MOBA = 8
N_HEADS_X = 4
MOBA_BLOCK = 256
MOBA_TOP_K = 3
ROPE_DIM = HEAD_DIM // 4
ROPE_THETA = 500000.0
EPS = 1e-6
ATTN_SCALE = HEAD_DIM ** -0.5
LOG2E = 1.4426950408889634

V7X_VMEM_BYTES = 64 * 1024 * 1024
VMEM_LIMIT_BYTES = V7X_VMEM_BYTES * 7 // 8

MM_TM = 1024
IN_PROJ_TN = 2048
IN_PROJ_ROW_CHUNKS = 4
FFN_TM = 2048
FFN_TN = 512
FFN_ROW_CHUNKS = 8
FUSED_TM = 512
DOWN_TM = 512
DOWN_ROW_CHUNKS = 2
ATT_TQ = 256
SB_HEADS_PER_STEP = 2
MOBA_HEADS_PER_STEP = 2
NEG_BIG = -1e30

_NT = (((1,), (1,)), ((), ()))


def _params(n_axes):
    return pltpu.CompilerParams(
        dimension_semantics=("arbitrary",) * n_axes,
        vmem_limit_bytes=VMEM_LIMIT_BYTES)


def _dot(a, b):
    return jnp.dot(a, b, preferred_element_type=F32)


def _dot_nt(a, b):
    return lax.dot_general(a, b, _NT, preferred_element_type=F32)


def _rmsnorm_rows(x, g):
    ms = jnp.mean(x * x, axis=-1, keepdims=True)
    return x * lax.rsqrt(ms + EPS) * g


def _norm_proj_kernel(x_ref, g_ref, w_ref, o_ref):
    h = _rmsnorm_rows(x_ref[...], g_ref[...]).astype(BF16)
    o_ref[...] = _dot(h, w_ref[...].astype(BF16)).astype(o_ref.dtype)


def _norm_proj(x, g, w):
    m, k = x.shape
    n = w.shape[1]
    tm = min(MM_TM, m)
    return pl.pallas_call(
        _norm_proj_kernel,
        out_shape=jax.ShapeDtypeStruct((m, n), BF16),
        grid=(m // tm,),
        in_specs=[pl.BlockSpec((tm, k), lambda i: (i, 0)),
                  pl.BlockSpec((1, k), lambda i: (0, 0)),
                  pl.BlockSpec((k, n), lambda i: (0, 0), pipeline_mode=pl.Buffered(1))],
        out_specs=pl.BlockSpec((tm, n), lambda i: (i, 0)),
        compiler_params=_params(1),
        name="mem_kv_proj",
    )(x, g.reshape(1, k), w)


def _in_proj_kernel(rot_heads, x_ref, g_ref, w_ref, s_ref, pos_ref, invf_ref, sign_ref, o_ref,
                    h_ref, cos_ref, sin_ref):
    tm, tn = o_ref.shape
    heads_per_tile = tn // HEAD_DIM
    assert not any(rot_heads[:heads_per_tile])

    def rope_tables():
        pack = HEAD_DIM // ROPE_DIM
        group = tm // pack
        lane = lax.broadcasted_iota(jnp.int32, (group, HEAD_DIM), 1)
        ang = jnp.zeros((group, HEAD_DIM), F32)
        for c in range(pack):
            pos_c = pos_ref[c * group:(c + 1) * group, :].astype(F32)
            in_block = (lane >= c * ROPE_DIM) & (lane < (c + 1) * ROPE_DIM)
            ang = jnp.where(in_block, pos_c * invf_ref[...], ang)
        cos_p = jnp.cos(ang)
        sin_p = jnp.sin(ang) * sign_ref[...]
        for c in range(pack):
            rows = slice(c * group, (c + 1) * group)
            shift = (HEAD_DIM - c * ROPE_DIM) % HEAD_DIM
            cos_c = cos_p if shift == 0 else pltpu.roll(cos_p, shift, axis=1)
            sin_c = sin_p if shift == 0 else pltpu.roll(sin_p, shift, axis=1)
            cos_ref[rows, :] = jnp.where(lane < ROPE_DIM, cos_c, 1.0)
            sin_ref[rows, :] = jnp.where(lane < ROPE_DIM, sin_c, 0.0)

    def column_step(c):
        chunk = tm // IN_PROJ_ROW_CHUNKS
        for r in range(IN_PROJ_ROW_CHUNKS):
            rows = slice(r * chunk, (r + 1) * chunk)
            if c == 0:
                h = _rmsnorm_rows(x_ref[rows, :], g_ref[...]).astype(h_ref.dtype)
                h_ref[rows, :] = h
            else:
                h = h_ref[rows, :]
            y = _dot(h, w_ref[...]) * s_ref[...]
            for hh in range(heads_per_tile):
                cols = slice(hh * HEAD_DIM, (hh + 1) * HEAD_DIM)
                y_h = y[:, cols]
                if rot_heads[c * heads_per_tile + hh]:
                    y_h = _rotate(y_h, cos_ref[rows, :], sin_ref[rows, :])
                o_ref[rows, cols] = y_h.astype(o_ref.dtype)
        if c == 0:
            rope_tables()

    for c in range(len(rot_heads) // heads_per_tile):
        pl.when(pl.program_id(1) == c)(functools.partial(column_step, c))


def _in_proj(x, g, w, col_scale, positions, rot_cols):
    m, k = x.shape
    n = w.shape[1]
    tm, tn = MM_TM, IN_PROJ_TN
    assert rot_cols[0] % HEAD_DIM == 0 and rot_cols[1] % HEAD_DIM == 0 and n % tn == 0
    rot_heads = tuple(rot_cols[0] <= c * HEAD_DIM < rot_cols[1] for c in range(n // HEAD_DIM))
    body = functools.partial(_in_proj_kernel, rot_heads)
    half = ROPE_DIM // 2
    pack = HEAD_DIM // ROPE_DIM
    inv_freq = ROPE_THETA ** (-jnp.arange(0, ROPE_DIM, 2, dtype=F32) / ROPE_DIM)
    invf = jnp.tile(jnp.concatenate([inv_freq, inv_freq]), pack).reshape(1, HEAD_DIM)
    sign = jnp.tile(jnp.concatenate([-jnp.ones((half,), F32), jnp.ones((half,), F32)]),
                    pack).reshape(1, HEAD_DIM)
    row_vec = pl.BlockSpec((1, HEAD_DIM), lambda i, j: (0, 0))
    return pl.pallas_call(
        body,
        out_shape=jax.ShapeDtypeStruct((m, n), BF16),
        grid=(m // tm, n // tn),
        in_specs=[pl.BlockSpec((tm, k), lambda i, j: (i, 0)),
                  pl.BlockSpec((1, k), lambda i, j: (0, 0)),
                  pl.BlockSpec((k, tn), lambda i, j: (0, j)),
                  pl.BlockSpec((1, tn), lambda i, j: (0, j)),
                  pl.BlockSpec((tm, 1), lambda i, j: (i, 0)),
                  row_vec, row_vec],
        out_specs=pl.BlockSpec((tm, tn), lambda i, j: (i, j)),
        scratch_shapes=[pltpu.VMEM((tm, k), BF16),
                        pltpu.VMEM((tm, HEAD_DIM), F32), pltpu.VMEM((tm, HEAD_DIM), F32)],
        compiler_params=_params(2),
        name="in_proj",
    )(x, g.reshape(1, k), w, col_scale.reshape(1, n), positions.reshape(m, 1), invf, sign)


def _mixer_xattn_kernel(osb_ref, omb_ref, wout_ref, x_ref, gx_ref, wq_ref, kv_ref, wo_ref, gf_ref,
                        x2_ref, h3_ref):
    k1 = osb_ref.shape[1]
    w = wq_ref.shape[1]
    half = x_ref.shape[0] // 2
    halves = [slice(r * half, (r + 1) * half) for r in range(2)]

    def out_proj(rows):
        return (x_ref[rows, :] + _dot(osb_ref[rows, :], wout_ref[:k1, :])
                + _dot(omb_ref[rows, :], wout_ref[k1:, :]))

    def q_proj(x1):
        h2 = _rmsnorm_rows(x1, gx_ref[...]).astype(BF16)
        return (_dot(h2, wq_ref[...]) * (ATTN_SCALE * LOG2E)).astype(BF16)

    def attend(q):
        heads = []
        for hh in range(w // HEAD_DIM):
            cols = slice(hh * HEAD_DIM, (hh + 1) * HEAD_DIM)
            s = _dot_nt(q[:, cols], kv_ref[0, :, cols])
            p = jnp.exp2(s - jnp.max(s, axis=-1, keepdims=True))
            l = jnp.sum(p, axis=-1, keepdims=True)
            v = kv_ref[0, :, w + hh * HEAD_DIM: w + (hh + 1) * HEAD_DIM]
            heads.append((_dot(p.astype(BF16), v) / l).astype(BF16))
        return jnp.concatenate(heads, axis=1)

    x1 = [out_proj(rows) for rows in halves]
    q = [q_proj(v) for v in x1]
    ox = [attend(v) for v in q]
    for rows, x1_r, ox_r in zip(halves, x1, ox):
        x2 = x1_r + _dot(ox_r, wo_ref[...])
        x2_ref[rows, :] = x2
        h3_ref[rows, :] = _rmsnorm_rows(x2, gf_ref[...]).astype(h3_ref.dtype)


def _mixer_out_xattn(o_sb, o_mb, w_out, x, g_xattn, w_xq, kv, w_xo, g_ffn, rows_per_batch):
    m, d = x.shape
    k1 = o_sb.shape[1]
    wx = w_xq.shape[1]
    n_mem = kv.shape[1]
    tm = FUSED_TM
    assert rows_per_batch % tm == 0
    tiles_per_batch = rows_per_batch // tm
    const = lambda shape: pl.BlockSpec(shape, lambda i: (0,) * len(shape), pipeline_mode=pl.Buffered(1))
    rows = lambda width: pl.BlockSpec((tm, width), lambda i: (i, 0))
    return pl.pallas_call(
        _mixer_xattn_kernel,
        out_shape=(jax.ShapeDtypeStruct((m, d), F32), jax.ShapeDtypeStruct((m, d), BF16)),
        grid=(m // tm,),
        in_specs=[rows(k1), rows(k1), const((2 * k1, d)), rows(d), const((1, d)), const((d, wx)),
                  pl.BlockSpec((1, n_mem, 2 * wx), lambda i: (i // tiles_per_batch, 0, 0)),
                  const((wx, d)), const((1, d))],
        out_specs=(rows(d), rows(d)),
        compiler_params=_params(1),
        name="mixer_out_xattn",
    )(o_sb, o_mb, w_out, x, g_xattn.reshape(1, d), w_xq, kv, w_xo, g_ffn.reshape(1, d))


def _down_final_kernel(a_ref, w_ref, r_ref, g_ref, o_ref):
    chunk = a_ref.shape[0] // DOWN_ROW_CHUNKS
    for r in range(DOWN_ROW_CHUNKS):
        rows = slice(r * chunk, (r + 1) * chunk)
        x3 = r_ref[rows, :] + _dot(a_ref[rows, :], w_ref[...])
        o_ref[rows, :] = _rmsnorm_rows(x3, g_ref[...])


def _down_final(a, w, residual, g):
    m, k = a.shape
    n = w.shape[1]
    tm = DOWN_TM
    return pl.pallas_call(
        _down_final_kernel,
        out_shape=jax.ShapeDtypeStruct((m, n), F32),
        grid=(m // tm,),
        in_specs=[pl.BlockSpec((tm, k), lambda i: (i, 0)),
                  pl.BlockSpec((k, n), lambda i: (0, 0), pipeline_mode=pl.Buffered(1)),
                  pl.BlockSpec((tm, n), lambda i: (i, 0)),
                  pl.BlockSpec((1, n), lambda i: (0, 0))],
        out_specs=pl.BlockSpec((tm, n), lambda i: (i, 0)),
        compiler_params=_params(1),
        name="down_final",
    )(a, w, residual, g.reshape(1, n))


def _swiglu_kernel(a_ref, wg_ref, wu_ref, o_ref, w_bf):
    tn = wg_ref.shape[1]

    @pl.when(pl.program_id(1) == 0)
    def _():
        w_bf[:, :tn] = wg_ref[...].astype(BF16)
        w_bf[:, tn:] = wu_ref[...].astype(BF16)

    chunk = a_ref.shape[0] // FFN_ROW_CHUNKS
    for r in range(FFN_ROW_CHUNKS):
        rows = slice(r * chunk, (r + 1) * chunk)
        gu = _dot(a_ref[rows, :], w_bf[...])
        g, u = gu[:, :tn], gu[:, tn:]
        o_ref[rows, :] = (g / (1.0 + jnp.exp(-g)) * u).astype(o_ref.dtype)


def _swiglu_up(h, w_gate_up):
    m, k = h.shape
    d_ff = w_gate_up.shape[1] // 2
    tm, tn = FFN_TM, FFN_TN
    nj = d_ff // tn
    return pl.pallas_call(
        _swiglu_kernel,
        out_shape=jax.ShapeDtypeStruct((m, d_ff), BF16),
        grid=(nj, m // tm),
        in_specs=[pl.BlockSpec((tm, k), lambda j, i: (i, 0)),
                  pl.BlockSpec((k, tn), lambda j, i: (0, j)),
                  pl.BlockSpec((k, tn), lambda j, i: (0, j + nj))],
        out_specs=pl.BlockSpec((tm, tn), lambda j, i: (i, j)),
        scratch_shapes=[pltpu.VMEM((k, 2 * tn), BF16)],
        compiler_params=_params(2),
        name="swiglu_up",
    )(h, w_gate_up, w_gate_up)


def _rotate(x, cos, sin_signed):
    half = ROPE_DIM // 2
    lane = lax.broadcasted_iota(jnp.int32, x.shape, 1)
    partner = jnp.where(lane < half,
                        pltpu.roll(x, HEAD_DIM - half, axis=1),
                        pltpu.roll(x, half, axis=1))
    return x * cos + partner * sin_signed


def _sb_kernel(q_ref, k_ref, v_ref, g_ref, o_ref, kt_ref):
    t = ATT_TQ
    d = HEAD_DIM
    n_tiles = k_ref.shape[1] // t
    heads = k_ref.shape[2] // d
    row = lax.broadcasted_iota(jnp.int32, (t, t), 0)
    col = lax.broadcasted_iota(jnp.int32, (t, t), 1)
    past = col < row
    neg_from = jnp.where(row >= col, -1.0, 0.0).astype(BF16)

    for j in range(n_tiles):
        for hh in range(heads):
            kt_ref[hh, :, j * t:(j + 1) * t] = (
                k_ref[0, j * t:(j + 1) * t, hh * d:(hh + 1) * d].astype(F32).T.astype(BF16))

    def logits(hh, i):
        q = q_ref[0, i * t:(i + 1) * t, hh * d:(hh + 1) * d]
        z, sp = [], []
        for j in range(i + 1):
            z_j = _dot(q, kt_ref[hh, :, j * t:(j + 1) * t])
            sp_j = jnp.maximum(z_j, 0.0) + jnp.log2(1.0 + jnp.exp2(-jnp.abs(z_j)))
            if j == i:
                sp_j = jnp.where(past, sp_j, 0.0)
            z.append(z_j)
            sp.append(sp_j.astype(BF16))
        return z, sp

    def weigh(hh, i, z, sp):
        cols = slice(hh * d, (hh + 1) * d)
        carry = jnp.zeros((t, 1), F32)
        acc = None
        for j in reversed(range(i + 1)):
            tail = _dot(sp[j], neg_from)
            w_j = jnp.exp2(z[j] + tail + carry)
            if j == i:
                w_j = jnp.where(past, w_j, 0.0)
            pv = _dot(w_j.astype(BF16), v_ref[0, j * t:(j + 1) * t, cols])
            acc = pv if acc is None else acc + pv
            carry = carry + tail[:, 0:1]
        o_ref[0, i * t:(i + 1) * t, cols] = _rmsnorm_rows(acc, g_ref[:, cols]).astype(o_ref.dtype)

    items = [(hh, i if hh % 2 == 0 else n_tiles - 1 - i) for i in range(n_tiles) for hh in range(heads)]
    pending = None
    for item in items:
        z, sp = logits(*item)
        if pending is not None:
            weigh(*pending)
        pending = (*item, z, sp)
    weigh(*pending)


def _sb_attention(qkv, g_out):
    b, s, _ = qkv.shape
    h = N_HEADS_SB
    hps = SB_HEADS_PER_STEP
    groups = h // hps
    width = hps * HEAD_DIM
    blk = lambda base: pl.BlockSpec((1, s, width), lambda bi, gi: (bi, 0, base + gi))
    return pl.pallas_call(
        _sb_kernel,
        out_shape=jax.ShapeDtypeStruct((b, s, h * HEAD_DIM), BF16),
        grid=(b, groups),
        in_specs=[blk(0), blk(groups), blk(2 * groups),
                  pl.BlockSpec((1, width), lambda bi, gi: (0, gi))],
        out_specs=blk(0),
        scratch_shapes=[pltpu.VMEM((hps, HEAD_DIM, s), BF16)],
        compiler_params=_params(2),
        name="sb_attention",
    )(qkv, qkv, qkv, g_out.reshape(1, h * HEAD_DIM))


def _moba_kernel(q_ref, k_ref, v_ref, g_ref, o_ref, ka_ref, km_ref):
    blk = MOBA_BLOCK
    nb = k_ref.shape[1] // blk
    d = HEAD_DIM
    heads = k_ref.shape[2] // d
    row = lax.broadcasted_iota(jnp.int32, (blk, blk), 0)
    col = lax.broadcasted_iota(jnp.int32, (blk, blk), 1)
    causal = col <= row
    sub = lax.broadcasted_iota(jnp.int32, (d, blk), 0)
    blk_id = lax.broadcasted_iota(jnp.int32, (km_ref.shape[1], blk), 0)

    km_ref[...] = jnp.zeros_like(km_ref)
    for j in range(nb):
        rows = slice(j * blk, (j + 1) * blk)
        for hh in range(heads):
            cols = slice(hh * d, (hh + 1) * d)
            kr = k_ref[0, rows, cols].astype(F32)
            ka_ref[hh, :d, rows] = kr.T.astype(BF16)
            ka_ref[hh, d:, rows] = jnp.where(sub == j, 1.0, 0.0).astype(BF16)
            km_ref[hh, j:j + 1, :] = jnp.mean(kr, axis=0, keepdims=True)

    def scores(hh, n):
        cols = slice(hh * d, (hh + 1) * d)
        rows = slice(n * blk, (n + 1) * blk)
        keys = (n + 1) * blk
        q = q_ref[0, rows, cols]
        if n <= MOBA_TOP_K:
            lhs, k_rows = q, slice(0, d)
        else:
            km = km_ref[hh]
            km_hi = km.astype(BF16)
            km_lo = (km - km_hi.astype(F32)).astype(BF16)
            gate = _dot_nt(km_hi, q) + _dot_nt(km_lo, q)
            rank = jnp.zeros(gate.shape, F32)
            for c in range(n):
                gc = gate[c:c + 1, :]
                beats = (gc > gate) | ((gc == gate) & (c < blk_id))
                rank = rank + jnp.where(beats, 1.0, 0.0)
            allowed = ((rank < MOBA_TOP_K) & (blk_id < n)) | (blk_id == n)
            bias_t = jnp.where(allowed, 0.0, NEG_BIG)
            bias_t = jnp.concatenate([bias_t, jnp.zeros((d - bias_t.shape[0], blk), F32)], axis=0)
            lhs = jnp.concatenate([q, bias_t.T.astype(BF16)], axis=1)
            k_rows = slice(0, 2 * d)
        s = []
        m_elem = None
        for j in range(n + 1):
            s_j = _dot(lhs, ka_ref[hh, k_rows, j * blk:(j + 1) * blk])
            if j == n:
                s_j = jnp.where(causal, s_j, -jnp.inf)
            s.append(s_j)
            m_elem = s_j if m_elem is None else jnp.maximum(m_elem, s_j)
        return s, jnp.max(m_elem, axis=-1, keepdims=True)

    def attend(hh, n, s, m):
        cols = slice(hh * d, (hh + 1) * d)
        rows = slice(n * blk, (n + 1) * blk)
        l_elem = None
        acc = None
        for j in range(n + 1):
            p_j = jnp.exp2(s[j] - m)
            pv = _dot(p_j.astype(BF16), v_ref[0, j * blk:(j + 1) * blk, cols])
            l_elem = p_j if l_elem is None else l_elem + p_j
            acc = pv if acc is None else acc + pv
        l = jnp.sum(l_elem, axis=-1, keepdims=True)
        o_ref[0, rows, cols] = _rmsnorm_rows(acc / l, g_ref[:, cols]).astype(o_ref.dtype)

    items = [(hh, n if hh % 2 == 0 else nb - 1 - n) for n in range(nb) for hh in range(heads)]
    pending = None
    for item in items:
        s, m = scores(*item)
        if pending is not None:
            attend(*pending)
        pending = (*item, s, m)
    attend(*pending)


def _moba_attention(qkv, g_out):
    b, s, _ = qkv.shape
    h = N_HEADS_MOBA
    hps = MOBA_HEADS_PER_STEP
    groups = h // hps
    width = hps * HEAD_DIM
    base = 3 * N_HEADS_SB // hps
    km_rows = 16
    assert s % MOBA_BLOCK == 0 and s // MOBA_BLOCK <= km_rows
    blk = lambda off: pl.BlockSpec((1, s, width), lambda bi, gi: (bi, 0, off + gi))
    return pl.pallas_call(
        _moba_kernel,
        out_shape=jax.ShapeDtypeStruct((b, s, h * HEAD_DIM), BF16),
        grid=(b, groups),
        in_specs=[blk(base), blk(base + groups), blk(base + 2 * groups),
                  pl.BlockSpec((1, width), lambda bi, gi: (0, gi))],
        out_specs=blk(0),
        scratch_shapes=[pltpu.VMEM((hps, 2 * HEAD_DIM, s), BF16),
                        pltpu.VMEM((hps, km_rows, HEAD_DIM), F32)],
        compiler_params=_params(2),
        name="moba_attention",
    )(qkv, qkv, qkv, g_out.reshape(1, h * HEAD_DIM))


def kernel(x, mem, positions, g_mix, w_in, g_out_sb, g_out_moba, w_out, g_xattn, g_mem,
           w_xq, w_xkv, w_xo, g_ffn, w_gate_up, w_down, g_final):
    b, s, d = x.shape
    n_mem = mem.shape[1]
    n = b * s
    x0 = x.reshape(n, d)

    w_sb, w_mb = N_HEADS_SB * HEAD_DIM, N_HEADS_MOBA * HEAD_DIM
    q_scale = jnp.concatenate([jnp.full((w_sb,), ATTN_SCALE * LOG2E, F32), jnp.ones((2 * w_sb,), F32),
                               jnp.full((w_mb,), ATTN_SCALE * LOG2E, F32), jnp.ones((2 * w_mb,), F32)])
    rot_cols = (3 * w_sb, 3 * w_sb + 2 * w_mb)
    qkv = _in_proj(x0, g_mix, w_in.astype(BF16), q_scale, positions, rot_cols).reshape(b, s, -1)
    o_sb = _sb_attention(qkv, g_out_sb)
    o_mb = _moba_attention(qkv, g_out_moba)

    kv = _norm_proj(mem.reshape(b * n_mem, d), g_mem, w_xkv).reshape(b, n_mem, -1)

    x2, h = _mixer_out_xattn(o_sb.reshape(n, -1), o_mb.reshape(n, -1), w_out.astype(BF16), x0,
                             g_xattn, w_xq.astype(BF16), kv, w_xo.astype(BF16), g_ffn, s)

    a = _swiglu_up(h, w_gate_up)
    return _down_final(a, w_down.astype(BF16), x2, g_final).reshape(b, s, d)
```

```python
import functools

import jax
import jax.numpy as jnp
from jax import lax
from jax.experimental import pallas as pl
from jax.experimental.pallas import tpu as pltpu

F32 = jnp.float32
BF16 = jnp.bfloat16

HEAD_DIM = 128
N_HEADS_SB = 8
N_HEADS_MOBA = 8
N_HEADS_X = 4
MOBA_BLOCK = 256
MOBA_TOP_K = 3
ROPE_DIM = HEAD_DIM // 4
ROPE_THETA = 500000.0
EPS = 1e-6
ATTN_SCALE = HEAD_DIM ** -0.5
LOG2E = 1.4426950408889634

V7X_VMEM_BYTES = 64 * 1024 * 1024
VMEM_LIMIT_BYTES = V7X_VMEM_BYTES * 7 // 8

MM_TM = 1024
IN_PROJ_TN = 2048
IN_PROJ_ROW_CHUNKS = 4
FFN_TM = 2048
FFN_TN = 512
FFN_ROW_CHUNKS = 8
FUSED_TM = 512
DOWN_TM = 512
DOWN_ROW_CHUNKS = 2
ATT_TQ = 256
SB_HEADS_PER_STEP = 2
MOBA_HEADS_PER_STEP = 2
NEG_BIG = -1e30

_NT = (((1,), (1,)), ((), ()))


def _params(n_axes):
    return pltpu.CompilerParams(
        dimension_semantics=("arbitrary",) * n_axes,
        vmem_limit_bytes=VMEM_LIMIT_BYTES)


def _dot(a, b):
    return jnp.dot(a, b, preferred_element_type=F32)


def _dot_nt(a, b):
    return lax.dot_general(a, b, _NT, preferred_element_type=F32)


def _rmsnorm_rows(x, g):
    ms = jnp.mean(x * x, axis=-1, keepdims=True)
    return x * lax.rsqrt(ms + EPS) * g


def _norm_proj_kernel(x_ref, g_ref, w_ref, o_ref):
    h = _rmsnorm_rows(x_ref[...], g_ref[...]).astype(BF16)
    o_ref[...] = _dot(h, w_ref[...].astype(BF16)).astype(o_ref.dtype)


def _norm_proj(x, g, w):
    m, k = x.shape
    n = w.shape[1]
    tm = min(MM_TM, m)
    return pl.pallas_call(
        _norm_proj_kernel,
        out_shape=jax.ShapeDtypeStruct((m, n), BF16),
        grid=(m // tm,),
        in_specs=[pl.BlockSpec((tm, k), lambda i: (i, 0)),
                  pl.BlockSpec((1, k), lambda i: (0, 0)),
                  pl.BlockSpec((k, n), lambda i: (0, 0), pipeline_mode=pl.Buffered(1))],
        out_specs=pl.BlockSpec((tm, n), lambda i: (i, 0)),
        compiler_params=_params(1),
        name="mem_kv_proj",
    )(x, g.reshape(1, k), w)


def _in_proj_kernel(rot_heads, x_ref, g_ref, w_ref, s_ref, pos_ref, invf_ref, sign_ref, o_ref,
                    h_ref, cos_ref, sin_ref):
    tm, tn = o_ref.shape
    heads_per_tile = tn // HEAD_DIM
    assert not any(rot_heads[:heads_per_tile])

    def rope_tables():
        pack = HEAD_DIM // ROPE_DIM
        group = tm // pack
        lane = lax.broadcasted_iota(jnp.int32, (group, HEAD_DIM), 1)
        ang = jnp.zeros((group, HEAD_DIM), F32)
        for c in range(pack):
            pos_c = pos_ref[c * group:(c + 1) * group, :].astype(F32)
            in_block = (lane >= c * ROPE_DIM) & (lane < (c + 1) * ROPE_DIM)
            ang = jnp.where(in_block, pos_c * invf_ref[...], ang)
        cos_p = jnp.cos(ang)
        sin_p = jnp.sin(ang) * sign_ref[...]
        for c in range(pack):
            rows = slice(c * group, (c + 1) * group)
            shift = (HEAD_DIM - c * ROPE_DIM) % HEAD_DIM
            cos_c = cos_p if shift == 0 else pltpu.roll(cos_p, shift, axis=1)
            sin_c = sin_p if shift == 0 else pltpu.roll(sin_p, shift, axis=1)
            cos_ref[rows, :] = jnp.where(lane < ROPE_DIM, cos_c, 1.0)
            sin_ref[rows, :] = jnp.where(lane < ROPE_DIM, sin_c, 0.0)

    def column_step(c):
        chunk = tm // IN_PROJ_ROW_CHUNKS
        for r in range(IN_PROJ_ROW_CHUNKS):
            rows = slice(r * chunk, (r + 1) * chunk)
            if c == 0:
                h = _rmsnorm_rows(x_ref[rows, :], g_ref[...]).astype(h_ref.dtype)
                h_ref[rows, :] = h
            else:
                h = h_ref[rows, :]
            y = _dot(h, w_ref[...]) * s_ref[...]
            for hh in range(heads_per_tile):
                cols = slice(hh * HEAD_DIM, (hh + 1) * HEAD_DIM)
                y_h = y[:, cols]
                if rot_heads[c * heads_per_tile + hh]:
                    y_h = _rotate(y_h, cos_ref[rows, :], sin_ref[rows, :])
                o_ref[rows, cols] = y_h.astype(o_ref.dtype)
        if c == 0:
            rope_tables()

    for c in range(len(rot_heads) // heads_per_tile):
        pl.when(pl.program_id(1) == c)(functools.partial(column_step, c))


def _in_proj(x, g, w, col_scale, positions, rot_cols):
    m, k = x.shape
    n = w.shape[1]
    tm, tn = MM_TM, IN_PROJ_TN
    assert rot_cols[0] % HEAD_DIM == 0 and rot_cols[1] % HEAD_DIM == 0 and n % tn == 0
    rot_heads = tuple(rot_cols[0] <= c * HEAD_DIM < rot_cols[1] for c in range(n // HEAD_DIM))
    body = functools.partial(_in_proj_kernel, rot_heads)
    half = ROPE_DIM // 2
    pack = HEAD_DIM // ROPE_DIM
    inv_freq = ROPE_THETA ** (-jnp.arange(0, ROPE_DIM, 2, dtype=F32) / ROPE_DIM)
    invf = jnp.tile(jnp.concatenate([inv_freq, inv_freq]), pack).reshape(1, HEAD_DIM)
    sign = jnp.tile(jnp.concatenate([-jnp.ones((half,), F32), jnp.ones((half,), F32)]),
                    pack).reshape(1, HEAD_DIM)
    row_vec = pl.BlockSpec((1, HEAD_DIM), lambda i, j: (0, 0))
    return pl.pallas_call(
        body,
        out_shape=jax.ShapeDtypeStruct((m, n), BF16),
        grid=(m // tm, n // tn),
        in_specs=[pl.BlockSpec((tm, k), lambda i, j: (i, 0)),
                  pl.BlockSpec((1, k), lambda i, j: (0, 0)),
                  pl.BlockSpec((k, tn), lambda i, j: (0, j)),
                  pl.BlockSpec((1, tn), lambda i, j: (0, j)),
                  pl.BlockSpec((tm, 1), lambda i, j: (i, 0)),
                  row_vec, row_vec],
        out_specs=pl.BlockSpec((tm, tn), lambda i, j: (i, j)),
        scratch_shapes=[pltpu.VMEM((tm, k), BF16),
                        pltpu.VMEM((tm, HEAD_DIM), F32), pltpu.VMEM((tm, HEAD_DIM), F32)],
        compiler_params=_params(2),
        name="in_proj",
    )(x, g.reshape(1, k), w, col_scale.reshape(1, n), positions.reshape(m, 1), invf, sign)


def _mixer_xattn_kernel(osb_ref, omb_ref, wout_ref, x_ref, gx_ref, wq_ref, kv_ref, wo_ref, gf_ref,
                        x2_ref, h3_ref):
    k1 = osb_ref.shape[1]
    w = wq_ref.shape[1]
    half = x_ref.shape[0] // 2
    halves = [slice(r * half, (r + 1) * half) for r in range(2)]

    def out_proj(rows):
        return (x_ref[rows, :] + _dot(osb_ref[rows, :], wout_ref[:k1, :])
                + _dot(omb_ref[rows, :], wout_ref[k1:, :]))

    def q_proj(x1):
        h2 = _rmsnorm_rows(x1, gx_ref[...]).astype(BF16)
        return (_dot(h2, wq_ref[...]) * (ATTN_SCALE * LOG2E)).astype(BF16)

    def attend(q):
        heads = []
        for hh in range(w // HEAD_DIM):
            cols = slice(hh * HEAD_DIM, (hh + 1) * HEAD_DIM)
            s = _dot_nt(q[:, cols], kv_ref[0, :, cols])
            p = jnp.exp2(s - jnp.max(s, axis=-1, keepdims=True))
            l = jnp.sum(p, axis=-1, keepdims=True)
            v = kv_ref[0, :, w + hh * HEAD_DIM: w + (hh + 1) * HEAD_DIM]
            heads.append((_dot(p.astype(BF16), v) / l).astype(BF16))
        return jnp.concatenate(heads, axis=1)

    x1 = [out_proj(rows) for rows in halves]
    q = [q_proj(v) for v in x1]
    ox = [attend(v) for v in q]
    for rows, x1_r, ox_r in zip(halves, x1, ox):
        x2 = x1_r + _dot(ox_r, wo_ref[...])
        x2_ref[rows, :] = x2
        h3_ref[rows, :] = _rmsnorm_rows(x2, gf_ref[...]).astype(h3_ref.dtype)


def _mixer_out_xattn(o_sb, o_mb, w_out, x, g_xattn, w_xq, kv, w_xo, g_ffn, rows_per_batch):
    m, d = x.shape
    k1 = o_sb.shape[1]
    wx = w_xq.shape[1]
    n_mem = kv.shape[1]
    tm = FUSED_TM
    assert rows_per_batch % tm == 0
    tiles_per_batch = rows_per_batch // tm
    const = lambda shape: pl.BlockSpec(shape, lambda i: (0,) * len(shape), pipeline_mode=pl.Buffered(1))
    rows = lambda width: pl.BlockSpec((tm, width), lambda i: (i, 0))
    return pl.pallas_call(
        _mixer_xattn_kernel,
        out_shape=(jax.ShapeDtypeStruct((m, d), F32), jax.ShapeDtypeStruct((m, d), BF16)),
        grid=(m // tm,),
        in_specs=[rows(k1), rows(k1), const((2 * k1, d)), rows(d), const((1, d)), const((d, wx)),
                  pl.BlockSpec((1, n_mem, 2 * wx), lambda i: (i // tiles_per_batch, 0, 0)),
                  const((wx, d)), const((1, d))],
        out_specs=(rows(d), rows(d)),
        compiler_params=_params(1),
        name="mixer_out_xattn",
    )(o_sb, o_mb, w_out, x, g_xattn.reshape(1, d), w_xq, kv, w_xo, g_ffn.reshape(1, d))


def _down_final_kernel(a_ref, w_ref, r_ref, g_ref, o_ref):
    chunk = a_ref.shape[0] // DOWN_ROW_CHUNKS
    for r in range(DOWN_ROW_CHUNKS):
        rows = slice(r * chunk, (r + 1) * chunk)
        x3 = r_ref[rows, :] + _dot(a_ref[rows, :], w_ref[...])
        o_ref[rows, :] = _rmsnorm_rows(x3, g_ref[...])


def _down_final(a, w, residual, g):
    m, k = a.shape
    n = w.shape[1]
    tm = DOWN_TM
    return pl.pallas_call(
        _down_final_kernel,
        out_shape=jax.ShapeDtypeStruct((m, n), F32),
        grid=(m // tm,),
        in_specs=[pl.BlockSpec((tm, k), lambda i: (i, 0)),
                  pl.BlockSpec((k, n), lambda i: (0, 0), pipeline_mode=pl.Buffered(1)),
                  pl.BlockSpec((tm, n), lambda i: (i, 0)),
                  pl.BlockSpec((1, n), lambda i: (0, 0))],
        out_specs=pl.BlockSpec((tm, n), lambda i: (i, 0)),
        compiler_params=_params(1),
        name="down_final",
    )(a, w, residual, g.reshape(1, n))


def _swiglu_kernel(a_ref, wg_ref, wu_ref, wd_ref, o_ref, wd_bf_ref, w_bf):
    tn = wg_ref.shape[1]

    @pl.when(pl.program_id(1) == 0)
    def _():
        w_bf[:, :tn] = wg_ref[...].astype(BF16)
        w_bf[:, tn:] = wu_ref[...].astype(BF16)
        wd_bf_ref[...] = wd_ref[...].astype(BF16)

    chunk = a_ref.shape[0] // FFN_ROW_CHUNKS
    for r in range(FFN_ROW_CHUNKS):
        rows = slice(r * chunk, (r + 1) * chunk)
        gu = _dot(a_ref[rows, :], w_bf[...])
        g, u = gu[:, :tn], gu[:, tn:]
        o_ref[rows, :] = (g / (1.0 + jnp.exp(-g)) * u).astype(o_ref.dtype)


def _swiglu_up(h, w_gate_up, w_down):
    m, k = h.shape
    d_ff, n_out = w_down.shape
    assert w_gate_up.shape[1] == 2 * d_ff
    tm, tn = FFN_TM, FFN_TN
    nj = d_ff // tn
    return pl.pallas_call(
        _swiglu_kernel,
        out_shape=(jax.ShapeDtypeStruct((m, d_ff), BF16), jax.ShapeDtypeStruct((d_ff, n_out), BF16)),
        grid=(nj, m // tm),
        in_specs=[pl.BlockSpec((tm, k), lambda j, i: (i, 0)),
                  pl.BlockSpec((k, tn), lambda j, i: (0, j)),
                  pl.BlockSpec((k, tn), lambda j, i: (0, j + nj)),
                  pl.BlockSpec((tn, n_out), lambda j, i: (j, 0))],
        out_specs=(pl.BlockSpec((tm, tn), lambda j, i: (i, j)),
                   pl.BlockSpec((tn, n_out), lambda j, i: (j, 0))),
        scratch_shapes=[pltpu.VMEM((k, 2 * tn), BF16)],
        compiler_params=_params(2),
        name="swiglu_up",
    )(h, w_gate_up, w_gate_up, w_down)


def _rotate(x, cos, sin_signed):
    half = ROPE_DIM // 2
    lane = lax.broadcasted_iota(jnp.int32, x.shape, 1)
    partner = jnp.where(lane < half,
                        pltpu.roll(x, HEAD_DIM - half, axis=1),
                        pltpu.roll(x, half, axis=1))
    return x * cos + partner * sin_signed


def _sb_kernel(q_ref, k_ref, v_ref, g_ref, o_ref, kt_ref):
    t = ATT_TQ
    d = HEAD_DIM
    n_tiles = k_ref.shape[1] // t
    heads = k_ref.shape[2] // d
    row = lax.broadcasted_iota(jnp.int32, (t, t), 0)
    col = lax.broadcasted_iota(jnp.int32, (t, t), 1)
    past = col < row
    neg_from = jnp.where(row >= col, -1.0, 0.0).astype(BF16)

    for j in range(n_tiles):
        for hh in range(heads):
            kt_ref[hh, :, j * t:(j + 1) * t] = (
                k_ref[0, j * t:(j + 1) * t, hh * d:(hh + 1) * d].astype(F32).T.astype(BF16))

    def logits(hh, i):
        q = q_ref[0, i * t:(i + 1) * t, hh * d:(hh + 1) * d]
        z, sp = [], []
        for j in range(i + 1):
            z_j = _dot(q, kt_ref[hh, :, j * t:(j + 1) * t])
            sp_j = jnp.maximum(z_j, 0.0) + jnp.log2(1.0 + jnp.exp2(-jnp.abs(z_j)))
            if j == i:
                sp_j = jnp.where(past, sp_j, 0.0)
            z.append(z_j)
            sp.append(sp_j.astype(BF16))
        return z, sp

    def weigh(hh, i, z, sp):
        cols = slice(hh * d, (hh + 1) * d)
        carry = jnp.zeros((t, 1), F32)
        acc = None
        for j in reversed(range(i + 1)):
            tail = _dot(sp[j], neg_from)
            w_j = jnp.exp2(z[j] + tail + carry)
            if j == i:
                w_j = jnp.where(past, w_j, 0.0)
            pv = _dot(w_j.astype(BF16), v_ref[0, j * t:(j + 1) * t, cols])
            acc = pv if acc is None else acc + pv
            carry = carry + tail[:, 0:1]
        o_ref[0, i * t:(i + 1) * t, cols] = _rmsnorm_rows(acc, g_ref[:, cols]).astype(o_ref.dtype)

    items = [(hh, i if hh % 2 == 0 else n_tiles - 1 - i) for i in range(n_tiles) for hh in range(heads)]
    pending = None
    for item in items:
        z, sp = logits(*item)
        if pending is not None:
            weigh(*pending)
        pending = (*item, z, sp)
    weigh(*pending)


def _sb_attention(qkv, g_out):
    b, s, _ = qkv.shape
    h = N_HEADS_SB
    hps = SB_HEADS_PER_STEP
    groups = h // hps
    width = hps * HEAD_DIM
    blk = lambda base: pl.BlockSpec((1, s, width), lambda bi, gi: (bi, 0, base + gi))
    return pl.pallas_call(
        _sb_kernel,
        out_shape=jax.ShapeDtypeStruct((b, s, h * HEAD_DIM), BF16),
        grid=(b, groups),
        in_specs=[blk(0), blk(groups), blk(2 * groups),
                  pl.BlockSpec((1, width), lambda bi, gi: (0, gi))],
        out_specs=blk(0),
        scratch_shapes=[pltpu.VMEM((hps, HEAD_DIM, s), BF16)],
        compiler_params=_params(2),
        name="sb_attention",
    )(qkv, qkv, qkv, g_out.reshape(1, h * HEAD_DIM))


def _moba_kernel(q_ref, k_ref, v_ref, g_ref, o_ref, ka_ref, km_ref):
    blk = MOBA_BLOCK
    nb = k_ref.shape[1] // blk
    d = HEAD_DIM
    heads = k_ref.shape[2] // d
    row = lax.broadcasted_iota(jnp.int32, (blk, blk), 0)
    col = lax.broadcasted_iota(jnp.int32, (blk, blk), 1)
    causal = col <= row
    sub = lax.broadcasted_iota(jnp.int32, (d, blk), 0)
    blk_id = lax.broadcasted_iota(jnp.int32, (km_ref.shape[1], blk), 0)

    km_ref[...] = jnp.zeros_like(km_ref)
    for j in range(nb):
        rows = slice(j * blk, (j + 1) * blk)
        for hh in range(heads):
            cols = slice(hh * d, (hh + 1) * d)
            kr = k_ref[0, rows, cols].astype(F32)
            ka_ref[hh, :d, rows] = kr.T.astype(BF16)
            ka_ref[hh, d:, rows] = jnp.where(sub == j, 1.0, 0.0).astype(BF16)
            km_ref[hh, j:j + 1, :] = jnp.mean(kr, axis=0, keepdims=True)

    def scores(hh, n):
        cols = slice(hh * d, (hh + 1) * d)
        rows = slice(n * blk, (n + 1) * blk)
        keys = (n + 1) * blk
        q = q_ref[0, rows, cols]
        if n <= MOBA_TOP_K:
            lhs, k_rows = q, slice(0, d)
        else:
            km = km_ref[hh]
            km_hi = km.astype(BF16)
            km_lo = (km - km_hi.astype(F32)).astype(BF16)
            gate = _dot_nt(km_hi, q) + _dot_nt(km_lo, q)
            rank = jnp.zeros(gate.shape, F32)
            for c in range(n):
                gc = gate[c:c + 1, :]
                beats = (gc > gate) | ((gc == gate) & (c < blk_id))
                rank = rank + jnp.where(beats, 1.0, 0.0)
            allowed = ((rank < MOBA_TOP_K) & (blk_id < n)) | (blk_id == n)
            bias_t = jnp.where(allowed, 0.0, NEG_BIG)
            bias_t = jnp.concatenate([bias_t, jnp.zeros((d - bias_t.shape[0], blk), F32)], axis=0)
            lhs = jnp.concatenate([q, bias_t.T.astype(BF16)], axis=1)
            k_rows = slice(0, 2 * d)
        s = []
        m_elem = None
        for j in range(n + 1):
            s_j = _dot(lhs, ka_ref[hh, k_rows, j * blk:(j + 1) * blk])
            if j == n:
                s_j = jnp.where(causal, s_j, -jnp.inf)
            s.append(s_j)
            m_elem = s_j if m_elem is None else jnp.maximum(m_elem, s_j)
        return s, jnp.max(m_elem, axis=-1, keepdims=True)

    def attend(hh, n, s, m):
        cols = slice(hh * d, (hh + 1) * d)
        rows = slice(n * blk, (n + 1) * blk)
        l_elem = None
        acc = None
        for j in range(n + 1):
            p_j = jnp.exp2(s[j] - m)
            pv = _dot(p_j.astype(BF16), v_ref[0, j * blk:(j + 1) * blk, cols])
            l_elem = p_j if l_elem is None else l_elem + p_j
            acc = pv if acc is None else acc + pv
        l = jnp.sum(l_elem, axis=-1, keepdims=True)
        o_ref[0, rows, cols] = _rmsnorm_rows(acc / l, g_ref[:, cols]).astype(o_ref.dtype)

    items = [(hh, n if hh % 2 == 0 else nb - 1 - n) for n in range(nb) for hh in range(heads)]
    pending = None
    for item in items:
        s, m = scores(*item)
        if pending is not None:
            attend(*pending)
        pending = (*item, s, m)
    attend(*pending)


def _moba_attention(qkv, g_out):
    b, s, _ = qkv.shape
    h = N_HEADS_MOBA
    hps = MOBA_HEADS_PER_STEP
    groups = h // hps
    width = hps * HEAD_DIM
    base = 3 * N_HEADS_SB // hps
    km_rows = 16
    assert s % MOBA_BLOCK == 0 and s // MOBA_BLOCK <= km_rows
    blk = lambda off: pl.BlockSpec((1, s, width), lambda bi, gi: (bi, 0, off + gi))
    return pl.pallas_call(
        _moba_kernel,
        out_shape=jax.ShapeDtypeStruct((b, s, h * HEAD_DIM), BF16),
        grid=(b, groups),
        in_specs=[blk(base), blk(base + groups), blk(base + 2 * groups),
                  pl.BlockSpec((1, width), lambda bi, gi: (0, gi))],
        out_specs=blk(0),
        scratch_shapes=[pltpu.VMEM((hps, 2 * HEAD_DIM, s), BF16),
                        pltpu.VMEM((hps, km_rows, HEAD_DIM), F32)],
        compiler_params=_params(2),
        name="moba_attention",
    )(qkv, qkv, qkv, g_out.reshape(1, h * HEAD_DIM))


def kernel(x, mem, positions, g_mix, w_in, g_out_sb, g_out_moba, w_out, g_xattn, g_mem,
           w_xq, w_xkv, w_xo, g_ffn, w_gate_up, w_down, g_final):
    b, s, d = x.shape
    n_mem = mem.shape[1]
    n = b * s
    x0 = x.reshape(n, d)

    w_sb, w_mb = N_HEADS_SB * HEAD_DIM, N_HEADS_MOBA * HEAD_DIM
    q_scale = jnp.concatenate([jnp.full((w_sb,), ATTN_SCALE * LOG2E, F32), jnp.ones((2 * w_sb,), F32),
                               jnp.full((w_mb,), ATTN_SCALE * LOG2E, F32), jnp.ones((2 * w_mb,), F32)])
    rot_cols = (3 * w_sb, 3 * w_sb + 2 * w_mb)
    qkv = _in_proj(x0, g_mix, w_in.astype(BF16), q_scale, positions, rot_cols).reshape(b, s, -1)
    o_sb = _sb_attention(qkv, g_out_sb)
    o_mb = _moba_attention(qkv, g_out_moba)

    kv = _norm_proj(mem.reshape(b * n_mem, d), g_mem, w_xkv).reshape(b, n_mem, -1)

    x2, h = _mixer_out_xattn(o_sb.reshape(n, -1), o_mb.reshape(n, -1), w_out.astype(BF16), x0,
                             g_xattn, w_xq.astype(BF16), kv, w_xo.astype(BF16), g_ffn, s)

    a, w_down_bf = _swiglu_up(h, w_gate_up, w_down)
    return _down_final(a, w_down_bf, x2, g_final).reshape(b, s, d)
```

```python
import functools

import jax
import jax.numpy as jnp
from jax import lax
from jax.experimental import pallas as pl
from jax.experimental.pallas import tpu as pltpu

F32 = jnp.float32
BF16 = jnp.bfloat16

HEAD_DIM = 128
N_HEADS_SB = 8
N_HEADS_MOBA = 8
MOBA_BLOCK = 256
MOBA_TOP_K = 3
ROPE_DIM = HEAD_DIM // 4
ROPE_THETA = 500000.0
EPS = 1e-6
ATTN_SCALE = HEAD_DIM ** -0.5
LOG2E = 1.4426950408889634

V7X_VMEM_BYTES = 64 * 1024 * 1024
VMEM_LIMIT_BYTES = V7X_VMEM_BYTES * 7 // 8

MM_TM = 1024
IN_PROJ_TN = 2048
IN_PROJ_ROW_CHUNKS = 4
FFN_TM = 2048
FFN_TN = 512
FFN_ROW_CHUNKS = 8
FUSED_TM = 512
DOWN_TM = 512
DOWN_ROW_CHUNKS = 2
ATT_TQ = 256
SB_HEADS_PER_STEP = 2
MOBA_HEADS_PER_STEP = 2
NEG_BIG = -1e30

_NT = (((1,), (1,)), ((), ()))


def _params(n_axes):
    return pltpu.CompilerParams(
        dimension_semantics=("arbitrary",) * n_axes,
        vmem_limit_bytes=VMEM_LIMIT_BYTES)


def _dot(a, b):
    return jnp.dot(a, b, preferred_element_type=F32)


def _dot_nt(a, b):
    return lax.dot_general(a, b, _NT, preferred_element_type=F32)


def _rmsnorm_rows(x, g):
    ms = jnp.mean(x * x, axis=-1, keepdims=True)
    return x * lax.rsqrt(ms + EPS) * g


def _norm_proj_kernel(x_ref, g_ref, w_ref, o_ref):
    h = _rmsnorm_rows(x_ref[...], g_ref[...]).astype(BF16)
    o_ref[...] = _dot(h, w_ref[...].astype(BF16)).astype(o_ref.dtype)


def _norm_proj(x, g, w):
    m, k = x.shape
    n = w.shape[1]
    tm = min(MM_TM, m)
    return pl.pallas_call(
        _norm_proj_kernel,
        out_shape=jax.ShapeDtypeStruct((m, n), BF16),
        grid=(m // tm,),
        in_specs=[pl.BlockSpec((tm, k), lambda i: (i, 0)),
                  pl.BlockSpec((1, k), lambda i: (0, 0)),
                  pl.BlockSpec((k, n), lambda i: (0, 0), pipeline_mode=pl.Buffered(1))],
        out_specs=pl.BlockSpec((tm, n), lambda i: (i, 0)),
        compiler_params=_params(1),
        name="mem_kv_proj",
    )(x, g.reshape(1, k), w)


def _in_proj_kernel(rot_heads, x_ref, g_ref, w_ref, s_ref, pos_ref, invf_ref, sign_ref, o_ref,
                    h_ref, cos_ref, sin_ref):
    tm, tn = o_ref.shape
    heads_per_tile = tn // HEAD_DIM
    assert not any(rot_heads[:heads_per_tile])

    def rope_tables():
        pack = HEAD_DIM // ROPE_DIM
        group = tm // pack
        lane = lax.broadcasted_iota(jnp.int32, (group, HEAD_DIM), 1)
        ang = jnp.zeros((group, HEAD_DIM), F32)
        for c in range(pack):
            pos_c = pos_ref[c * group:(c + 1) * group, :].astype(F32)
            in_block = (lane >= c * ROPE_DIM) & (lane < (c + 1) * ROPE_DIM)
            ang = jnp.where(in_block, pos_c * invf_ref[...], ang)
        cos_p = jnp.cos(ang)
        sin_p = jnp.sin(ang) * sign_ref[...]
        for c in range(pack):
            rows = slice(c * group, (c + 1) * group)
            shift = (HEAD_DIM - c * ROPE_DIM) % HEAD_DIM
            cos_c = cos_p if shift == 0 else pltpu.roll(cos_p, shift, axis=1)
            sin_c = sin_p if shift == 0 else pltpu.roll(sin_p, shift, axis=1)
            cos_ref[rows, :] = jnp.where(lane < ROPE_DIM, cos_c, 1.0)
            sin_ref[rows, :] = jnp.where(lane < ROPE_DIM, sin_c, 0.0)

    def column_step(c):
        chunk = tm // IN_PROJ_ROW_CHUNKS
        for r in range(IN_PROJ_ROW_CHUNKS):
            rows = slice(r * chunk, (r + 1) * chunk)
            if c == 0:
                h = _rmsnorm_rows(x_ref[rows, :], g_ref[...]).astype(h_ref.dtype)
                h_ref[rows, :] = h
            else:
                h = h_ref[rows, :]
            y = _dot(h, w_ref[...]) * s_ref[...]
            for hh in range(heads_per_tile):
                cols = slice(hh * HEAD_DIM, (hh + 1) * HEAD_DIM)
                y_h = y[:, cols]
                if rot_heads[c * heads_per_tile + hh]:
                    y_h = _rotate(y_h, cos_ref[rows, :], sin_ref[rows, :])
                o_ref[rows, cols] = y_h.astype(o_ref.dtype)
        if c == 0:
            rope_tables()

    for c in range(len(rot_heads) // heads_per_tile):
        pl.when(pl.program_id(1) == c)(functools.partial(column_step, c))


def _in_proj(x, g, w, col_scale, positions, rot_cols):
    m, k = x.shape
    n = w.shape[1]
    tm, tn = MM_TM, IN_PROJ_TN
    assert rot_cols[0] % HEAD_DIM == 0 and rot_cols[1] % HEAD_DIM == 0 and n % tn == 0
    rot_heads = tuple(rot_cols[0] <= c * HEAD_DIM < rot_cols[1] for c in range(n // HEAD_DIM))
    body = functools.partial(_in_proj_kernel, rot_heads)
    half = ROPE_DIM // 2
    pack = HEAD_DIM // ROPE_DIM
    inv_freq = ROPE_THETA ** (-jnp.arange(0, ROPE_DIM, 2, dtype=F32) / ROPE_DIM)
    invf = jnp.tile(jnp.concatenate([inv_freq, inv_freq]), pack).reshape(1, HEAD_DIM)
    sign = jnp.tile(jnp.concatenate([-jnp.ones((half,), F32), jnp.ones((half,), F32)]),
                    pack).reshape(1, HEAD_DIM)
    row_vec = pl.BlockSpec((1, HEAD_DIM), lambda i, j: (0, 0))
    return pl.pallas_call(
        body,
        out_shape=jax.ShapeDtypeStruct((m, n), BF16),
        grid=(m // tm, n // tn),
        in_specs=[pl.BlockSpec((tm, k), lambda i, j: (i, 0)),
                  pl.BlockSpec((1, k), lambda i, j: (0, 0)),
                  pl.BlockSpec((k, tn), lambda i, j: (0, j)),
                  pl.BlockSpec((1, tn), lambda i, j: (0, j)),
                  pl.BlockSpec((tm, 1), lambda i, j: (i, 0)),
                  row_vec, row_vec],
        out_specs=pl.BlockSpec((tm, tn), lambda i, j: (i, j)),
        scratch_shapes=[pltpu.VMEM((tm, k), BF16),
                        pltpu.VMEM((tm, HEAD_DIM), F32), pltpu.VMEM((tm, HEAD_DIM), F32)],
        compiler_params=_params(2),
        name="in_proj",
    )(x, g.reshape(1, k), w, col_scale.reshape(1, n), positions.reshape(m, 1), invf, sign)


def _mixer_xattn_kernel(osb_ref, omb_ref, wout_ref, x_ref, gx_ref, wq_ref, kv_ref, wo_ref, gf_ref,
                        x2_ref, h3_ref):
    k1 = osb_ref.shape[1]
    w = wq_ref.shape[1]
    half = x_ref.shape[0] // 2
    halves = [slice(r * half, (r + 1) * half) for r in range(2)]

    def out_proj(rows):
        return (x_ref[rows, :] + _dot(osb_ref[rows, :], wout_ref[:k1, :])
                + _dot(omb_ref[rows, :], wout_ref[k1:, :]))

    def q_proj(x1):
        h2 = _rmsnorm_rows(x1, gx_ref[...]).astype(BF16)
        return (_dot(h2, wq_ref[...]) * (ATTN_SCALE * LOG2E)).astype(BF16)

    def attend(q):
        heads = []
        for hh in range(w // HEAD_DIM):
            cols = slice(hh * HEAD_DIM, (hh + 1) * HEAD_DIM)
            s = _dot_nt(q[:, cols], kv_ref[0, :, cols])
            p = jnp.exp2(s - jnp.max(s, axis=-1, keepdims=True))
            l = jnp.sum(p, axis=-1, keepdims=True)
            v = kv_ref[0, :, w + hh * HEAD_DIM: w + (hh + 1) * HEAD_DIM]
            heads.append((_dot(p.astype(BF16), v) / l).astype(BF16))
        return jnp.concatenate(heads, axis=1)

    x1 = [out_proj(rows) for rows in halves]
    q = [q_proj(v) for v in x1]
    ox = [attend(v) for v in q]
    for rows, x1_r, ox_r in zip(halves, x1, ox):
        x2 = x1_r + _dot(ox_r, wo_ref[...])
        x2_ref[rows, :] = x2
        h3_ref[rows, :] = _rmsnorm_rows(x2, gf_ref[...]).astype(h3_ref.dtype)


def _mixer_out_xattn(o_sb, o_mb, w_out, x, g_xattn, w_xq, kv, w_xo, g_ffn, rows_per_batch):
    m, d = x.shape
    k1 = o_sb.shape[1]
    wx = w_xq.shape[1]
    n_mem = kv.shape[1]
    tm = FUSED_TM
    assert rows_per_batch % tm == 0
    tiles_per_batch = rows_per_batch // tm
    const = lambda shape: pl.BlockSpec(shape, lambda i: (0,) * len(shape), pipeline_mode=pl.Buffered(1))
    rows = lambda width: pl.BlockSpec((tm, width), lambda i: (i, 0))
    return pl.pallas_call(
        _mixer_xattn_kernel,
        out_shape=(jax.ShapeDtypeStruct((m, d), F32), jax.ShapeDtypeStruct((m, d), BF16)),
        grid=(m // tm,),
        in_specs=[rows(k1), rows(k1), const((2 * k1, d)), rows(d), const((1, d)), const((d, wx)),
                  pl.BlockSpec((1, n_mem, 2 * wx), lambda i: (i // tiles_per_batch, 0, 0)),
                  const((wx, d)), const((1, d))],
        out_specs=(rows(d), rows(d)),
        compiler_params=_params(1),
        name="mixer_out_xattn",
    )(o_sb, o_mb, w_out, x, g_xattn.reshape(1, d), w_xq, kv, w_xo, g_ffn.reshape(1, d))


def _down_final_kernel(a_ref, w_ref, r_ref, g_ref, o_ref):
    chunk = a_ref.shape[0] // DOWN_ROW_CHUNKS
    for r in range(DOWN_ROW_CHUNKS):
        rows = slice(r * chunk, (r + 1) * chunk)
        x3 = r_ref[rows, :] + _dot(a_ref[rows, :], w_ref[...])
        o_ref[rows, :] = _rmsnorm_rows(x3, g_ref[...])


def _down_final(a, w, residual, g):
    m, k = a.shape
    n = w.shape[1]
    tm = DOWN_TM
    return pl.pallas_call(
        _down_final_kernel,
        out_shape=jax.ShapeDtypeStruct((m, n), F32),
        grid=(m // tm,),
        in_specs=[pl.BlockSpec((tm, k), lambda i: (i, 0)),
                  pl.BlockSpec((k, n), lambda i: (0, 0), pipeline_mode=pl.Buffered(1)),
                  pl.BlockSpec((tm, n), lambda i: (i, 0)),
                  pl.BlockSpec((1, n), lambda i: (0, 0))],
        out_specs=pl.BlockSpec((tm, n), lambda i: (i, 0)),
        compiler_params=_params(1),
        name="down_final",
    )(a, w, residual, g.reshape(1, n))


def _swiglu_kernel(a_ref, wg_ref, wu_ref, wd_ref, o_ref, wd_bf_ref, w_bf):
    tn = wg_ref.shape[1]

    @pl.when(pl.program_id(1) == 0)
    def _():
        w_bf[:, :tn] = wg_ref[...].astype(BF16)
        w_bf[:, tn:] = wu_ref[...].astype(BF16)
        wd_bf_ref[...] = wd_ref[...].astype(BF16)

    chunk = a_ref.shape[0] // FFN_ROW_CHUNKS
    for r in range(FFN_ROW_CHUNKS):
        rows = slice(r * chunk, (r + 1) * chunk)
        gu = _dot(a_ref[rows, :], w_bf[...])
        g, u = gu[:, :tn], gu[:, tn:]
        o_ref[rows, :] = (g / (1.0 + jnp.exp(-g)) * u).astype(o_ref.dtype)


def _swiglu_up(h, w_gate_up, w_down):
    m, k = h.shape
    d_ff, n_out = w_down.shape
    assert w_gate_up.shape[1] == 2 * d_ff
    tm, tn = FFN_TM, FFN_TN
    nj = d_ff // tn
    return pl.pallas_call(
        _swiglu_kernel,
        out_shape=(jax.ShapeDtypeStruct((m, d_ff), BF16), jax.ShapeDtypeStruct((d_ff, n_out), BF16)),
        grid=(nj, m // tm),
        in_specs=[pl.BlockSpec((tm, k), lambda j, i: (i, 0)),
                  pl.BlockSpec((k, tn), lambda j, i: (0, j)),
                  pl.BlockSpec((k, tn), lambda j, i: (0, j + nj)),
                  pl.BlockSpec((tn, n_out), lambda j, i: (j, 0))],
        out_specs=(pl.BlockSpec((tm, tn), lambda j, i: (i, j)),
                   pl.BlockSpec((tn, n_out), lambda j, i: (j, 0))),
        scratch_shapes=[pltpu.VMEM((k, 2 * tn), BF16)],
        compiler_params=_params(2),
        name="swiglu_up",
    )(h, w_gate_up, w_gate_up, w_down)


def _rotate(x, cos, sin_signed):
    half = ROPE_DIM // 2
    lane = lax.broadcasted_iota(jnp.int32, x.shape, 1)
    partner = jnp.where(lane < half,
                        pltpu.roll(x, HEAD_DIM - half, axis=1),
                        pltpu.roll(x, half, axis=1))
    return x * cos + partner * sin_signed


def _sb_kernel(q_ref, k_ref, v_ref, g_ref, o_ref, kt_ref):
    t = ATT_TQ
    d = HEAD_DIM
    n_tiles = k_ref.shape[1] // t
    heads = k_ref.shape[2] // d
    row = lax.broadcasted_iota(jnp.int32, (t, t), 0)
    col = lax.broadcasted_iota(jnp.int32, (t, t), 1)
    past = col < row
    neg_from = jnp.where(row >= col, -1.0, 0.0).astype(BF16)

    for j in range(n_tiles):
        for hh in range(heads):
            kt_ref[hh, :, j * t:(j + 1) * t] = (
                k_ref[0, j * t:(j + 1) * t, hh * d:(hh + 1) * d].astype(F32).T.astype(BF16))

    def logits(hh, i):
        q = q_ref[0, i * t:(i + 1) * t, hh * d:(hh + 1) * d]
        z, sp = [], []
        for j in range(i + 1):
            z_j = _dot(q, kt_ref[hh, :, j * t:(j + 1) * t])
            sp_j = jnp.maximum(z_j, 0.0) + jnp.log2(1.0 + jnp.exp2(-jnp.abs(z_j)))
            if j == i:
                sp_j = jnp.where(past, sp_j, 0.0)
            z.append(z_j)
            sp.append(sp_j.astype(BF16))
        return z, sp

    def weigh(hh, i, z, sp):
        cols = slice(hh * d, (hh + 1) * d)
        carry = jnp.zeros((t, 1), F32)
        acc = None
        for j in reversed(range(i + 1)):
            tail = _dot(sp[j], neg_from)
            w_j = jnp.exp2(z[j] + tail + carry)
            if j == i:
                w_j = jnp.where(past, w_j, 0.0)
            pv = _dot(w_j.astype(BF16), v_ref[0, j * t:(j + 1) * t, cols])
            acc = pv if acc is None else acc + pv
            carry = carry + tail[:, 0:1]
        o_ref[0, i * t:(i + 1) * t, cols] = _rmsnorm_rows(acc, g_ref[:, cols]).astype(o_ref.dtype)

    items = [(hh, i if hh % 2 == 0 else n_tiles - 1 - i) for i in range(n_tiles) for hh in range(heads)]
    pending = None
    for item in items:
        z, sp = logits(*item)
        if pending is not None:
            weigh(*pending)
        pending = (*item, z, sp)
    weigh(*pending)


def _sb_attention(qkv, g_out):
    b, s, _ = qkv.shape
    h = N_HEADS_SB
    hps = SB_HEADS_PER_STEP
    groups = h // hps
    width = hps * HEAD_DIM
    blk = lambda base: pl.BlockSpec((1, s, width), lambda bi, gi: (bi, 0, base + gi))
    return pl.pallas_call(
        _sb_kernel,
        out_shape=jax.ShapeDtypeStruct((b, s, h * HEAD_DIM), BF16),
        grid=(b, groups),
        in_specs=[blk(0), blk(groups), blk(2 * groups),
                  pl.BlockSpec((1, width), lambda bi, gi: (0, gi))],
        out_specs=blk(0),
        scratch_shapes=[pltpu.VMEM((hps, HEAD_DIM, s), BF16)],
        compiler_params=_params(2),
        name="sb_attention",
    )(qkv, qkv, qkv, g_out.reshape(1, h * HEAD_DIM))


def _moba_kernel(q_ref, k_ref, v_ref, g_ref, o_ref, ka_ref, km_ref):
    blk = MOBA_BLOCK
    nb = k_ref.shape[1] // blk
    d = HEAD_DIM
    heads = k_ref.shape[2] // d
    row = lax.broadcasted_iota(jnp.int32, (blk, blk), 0)
    col = lax.broadcasted_iota(jnp.int32, (blk, blk), 1)
    causal = col <= row
    sub = lax.broadcasted_iota(jnp.int32, (d, blk), 0)
    blk_id = lax.broadcasted_iota(jnp.int32, (km_ref.shape[1], blk), 0)

    km_ref[...] = jnp.zeros_like(km_ref)
    for j in range(nb):
        rows = slice(j * blk, (j + 1) * blk)
        for hh in range(heads):
            cols = slice(hh * d, (hh + 1) * d)
            kr = k_ref[0, rows, cols].astype(F32)
            ka_ref[hh, :d, rows] = kr.T.astype(BF16)
            ka_ref[hh, d:, rows] = jnp.where(sub == j, 1.0, 0.0).astype(BF16)
            km_ref[hh, j:j + 1, :] = jnp.mean(kr, axis=0, keepdims=True)

    def scores(hh, n):
        cols = slice(hh * d, (hh + 1) * d)
        rows = slice(n * blk, (n + 1) * blk)
        keys = (n + 1) * blk
        q = q_ref[0, rows, cols]
        if n <= MOBA_TOP_K:
            lhs, k_rows = q, slice(0, d)
        else:
            km = km_ref[hh]
            km_hi = km.astype(BF16)
            km_lo = (km - km_hi.astype(F32)).astype(BF16)
            gate = _dot_nt(km_hi, q) + _dot_nt(km_lo, q)
            rank = jnp.zeros(gate.shape, F32)
            for c in range(n):
                gc = gate[c:c + 1, :]
                beats = (gc > gate) | ((gc == gate) & (c < blk_id))
                rank = rank + jnp.where(beats, 1.0, 0.0)
            allowed = ((rank < MOBA_TOP_K) & (blk_id < n)) | (blk_id == n)
            bias_t = jnp.where(allowed, 0.0, NEG_BIG)
            bias_t = jnp.concatenate([bias_t, jnp.zeros((d - bias_t.shape[0], blk), F32)], axis=0)
            lhs = jnp.concatenate([q, bias_t.T.astype(BF16)], axis=1)
            k_rows = slice(0, 2 * d)
        s = []
        m_elem = None
        for j in range(n + 1):
            s_j = _dot(lhs, ka_ref[hh, k_rows, j * blk:(j + 1) * blk])
            if j == n:
                s_j = jnp.where(causal, s_j, -jnp.inf)
            s.append(s_j)
            m_elem = s_j if m_elem is None else jnp.maximum(m_elem, s_j)
        return s, jnp.max(m_elem, axis=-1, keepdims=True)

    def attend(hh, n, s, m):
        cols = slice(hh * d, (hh + 1) * d)
        rows = slice(n * blk, (n + 1) * blk)
        l_elem = None
        acc = None
        for j in range(n + 1):
            p_j = jnp.exp2(s[j] - m)
            pv = _dot(p_j.astype(BF16), v_ref[0, j * blk:(j + 1) * blk, cols])
            l_elem = p_j if l_elem is None else l_elem + p_j
            acc = pv if acc is None else acc + pv
        l = jnp.sum(l_elem, axis=-1, keepdims=True)
        o_ref[0, rows, cols] = _rmsnorm_rows(acc / l, g_ref[:, cols]).astype(o_ref.dtype)

    items = [(hh, n if hh % 2 == 0 else nb - 1 - n) for n in range(nb) for hh in range(heads)]
    pending = None
    for item in items:
        s, m = scores(*item)
        if pending is not None:
            attend(*pending)
        pending = (*item, s, m)
    attend(*pending)


def _moba_attention(qkv, g_out):
    b, s, _ = qkv.shape
    h = N_HEADS_MOBA
    hps = MOBA_HEADS_PER_STEP
    groups = h // hps
    width = hps * HEAD_DIM
    base = 3 * N_HEADS_SB // hps
    km_rows = 16
    assert s % MOBA_BLOCK == 0 and s // MOBA_BLOCK <= km_rows
    blk = lambda off: pl.BlockSpec((1, s, width), lambda bi, gi: (bi, 0, off + gi))
    return pl.pallas_call(
        _moba_kernel,
        out_shape=jax.ShapeDtypeStruct((b, s, h * HEAD_DIM), BF16),
        grid=(b, groups),
        in_specs=[blk(base), blk(base + groups), blk(base + 2 * groups),
                  pl.BlockSpec((1, width), lambda bi, gi: (0, gi))],
        out_specs=blk(0),
        scratch_shapes=[pltpu.VMEM((hps, 2 * HEAD_DIM, s), BF16),
                        pltpu.VMEM((hps, km_rows, HEAD_DIM), F32)],
        compiler_params=_params(2),
        name="moba_attention",
    )(qkv, qkv, qkv, g_out.reshape(1, h * HEAD_DIM))


def kernel(x, mem, positions, g_mix, w_in, g_out_sb, g_out_moba, w_out, g_xattn, g_mem,
           w_xq, w_xkv, w_xo, g_ffn, w_gate_up, w_down, g_final):
    b, s, d = x.shape
    n_mem = mem.shape[1]
    n = b * s
    x0 = x.reshape(n, d)

    w_sb, w_mb = N_HEADS_SB * HEAD_DIM, N_HEADS_MOBA * HEAD_DIM
    q_scale = jnp.concatenate([jnp.full((w_sb,), ATTN_SCALE * LOG2E, F32), jnp.ones((2 * w_sb,), F32),
                               jnp.full((w_mb,), ATTN_SCALE * LOG2E, F32), jnp.ones((2 * w_mb,), F32)])
    rot_cols = (3 * w_sb, 3 * w_sb + 2 * w_mb)
    qkv = _in_proj(x0, g_mix, w_in.astype(BF16), q_scale, positions, rot_cols).reshape(b, s, -1)
    o_sb = _sb_attention(qkv, g_out_sb)
    o_mb = _moba_attention(qkv, g_out_moba)

    kv = _norm_proj(mem.reshape(b * n_mem, d), g_mem, w_xkv).reshape(b, n_mem, -1)

    x2, h = _mixer_out_xattn(o_sb.reshape(n, -1), o_mb.reshape(n, -1), w_out.astype(BF16), x0,
                             g_xattn, w_xq.astype(BF16), kv, w_xo.astype(BF16), g_ffn, s)

    a, w_down_bf = _swiglu_up(h, w_gate_up, w_down)
    return _down_final(a, w_down_bf, x2, g_final).reshape(b, s, d)
```
